```python
import math
import jax, jax.numpy as jnp
from jax import lax
import numpy as np

D_MODEL = 1024
BATCH = 8
SEQ = 4096
DEPTH = 1

A_HEADS = 8
A_HEAD_DIM = 64
A_WIDTH = A_HEADS * A_HEAD_DIM
MOBA_BLOCK = 256
MOBA_TOPK = 3
MOBA_Q_CHUNK = 32
B_HEADS = 8
B_NOPE = 64
B_ROPE = 32
B_V = 64
B_WIDTH = B_HEADS * B_V
Q_LORA = 256
KV_LORA = 128
MLA_Q_BLOCK = 128
PEER_HEADS = 8
PEER_NKEYS = 128
PEER_EXPERTS = PEER_NKEYS * PEER_NKEYS
PEER_HALF = 128
PEER_QDIM = 2 * PEER_HALF
PEER_TOPK = 16
PEER_TOKEN_CHUNK = 128
ROPE_THETA = 10000.0
RMS_EPS = 1e-6
IN_WIDTHS = (A_WIDTH, A_WIDTH, A_WIDTH, Q_LORA, KV_LORA, B_ROPE, D_MODEL, D_MODEL)
IN_TOTAL = 3 * A_WIDTH + Q_LORA + KV_LORA + B_ROPE + 2 * D_MODEL

kernel_name = "hybrid_moba_mla_peer_block"


def rms_norm(x, g):
    xf = x.astype(jnp.float32)
    y = xf * lax.rsqrt(jnp.mean(xf * xf, axis=-1, keepdims=True) + RMS_EPS)
    return (y * g.astype(jnp.float32)).astype(x.dtype)


def rope_tables(positions, dim, dtype):
    inv_freq = ROPE_THETA ** (-jnp.arange(0, dim, 2, dtype=jnp.float32) / dim)
    ang = positions.astype(jnp.float32)[..., None] * inv_freq
    return jnp.cos(ang)[:, :, None, :].astype(dtype), jnp.sin(ang)[:, :, None, :].astype(dtype)


def apply_rope(x, cos, sin):
    x1, x2 = jnp.split(x, 2, axis=-1)
    return jnp.concatenate([x1 * cos - x2 * sin, x2 * cos + x1 * sin], axis=-1)


def moba_attention(q, k, v):
    B, S, H, dh = q.shape
    nb = -(-S // MOBA_BLOCK)
    pad = nb * MOBA_BLOCK - S
    k_eff = min(MOBA_TOPK, nb)
    scale = 1.0 / math.sqrt(dh)
    qt = q.transpose(0, 2, 1, 3)
    kt = jnp.pad(k.transpose(0, 2, 1, 3), ((0, 0), (0, 0), (0, pad), (0, 0)))
    vt = jnp.pad(v.transpose(0, 2, 1, 3), ((0, 0), (0, 0), (0, pad), (0, 0)))
    kb = kt.reshape(B, H, nb, MOBA_BLOCK, dh)
    vb = vt.reshape(B, H, nb, MOBA_BLOCK, dh)
    k_mean = jnp.mean(kb.astype(jnp.float32), axis=3)
    b_ix = jnp.arange(B)[:, None, None, None]
    h_ix = jnp.arange(H)[None, :, None, None]
    n_chunks = S // MOBA_Q_CHUNK

    def chunk(ci):
        start = ci * MOBA_Q_CHUNK
        blk = start // MOBA_BLOCK
        qc = lax.dynamic_slice_in_dim(qt, start, MOBA_Q_CHUNK, axis=2)
        gate = jnp.einsum('bhqd,bhnd->bhqn', qc.astype(jnp.float32), k_mean)
        gate = jnp.where(jnp.arange(nb) < blk, gate, -jnp.inf)
        _, sel = lax.top_k(gate, k_eff)
        sel_valid = jnp.arange(k_eff) < blk
        k_sel = kb[b_ix, h_ix, sel]
        v_sel = vb[b_ix, h_ix, sel]
        s_sel = jnp.einsum('bhqd,bhqtkd->bhqtk', qc, k_sel).astype(jnp.float32) * scale
        s_sel = jnp.where(sel_valid[:, None], s_sel, -jnp.inf)
        s_sel = s_sel.reshape(B, H, MOBA_Q_CHUNK, k_eff * MOBA_BLOCK)
        k_own = lax.dynamic_index_in_dim(kb, blk, axis=2, keepdims=False)
        v_own = lax.dynamic_index_in_dim(vb, blk, axis=2, keepdims=False)
        s_own = jnp.einsum('bhqd,bhkd->bhqk', qc, k_own).astype(jnp.float32) * scale
        q_pos = start + jnp.arange(MOBA_Q_CHUNK)
        k_pos = blk * MOBA_BLOCK + jnp.arange(MOBA_BLOCK)
        s_own = jnp.where(k_pos[None, :] <= q_pos[:, None], s_own, -jnp.inf)
        p = jax.nn.softmax(jnp.concatenate([s_sel, s_own], axis=-1), axis=-1).astype(v.dtype)
        p_sel = p[..., :k_eff * MOBA_BLOCK].reshape(B, H, MOBA_Q_CHUNK, k_eff, MOBA_BLOCK)
        p_own = p[..., k_eff * MOBA_BLOCK:]
        return (jnp.einsum('bhqtk,bhqtkd->bhqd', p_sel, v_sel)
                + jnp.einsum('bhqk,bhkd->bhqd', p_own, v_own))

    out = lax.map(chunk, jnp.arange(n_chunks))
    return out.transpose(1, 0, 3, 2, 4).reshape(B, S, H * dh)


def causal_attention_blocks(q, k, v, scale):
    B, H, S, _ = q.shape
    dv = v.shape[-1]
    k_pos = jnp.arange(S)

    def block(bi):
        qb = lax.dynamic_slice_in_dim(q, bi * MLA_Q_BLOCK, MLA_Q_BLOCK, axis=2)
        s = jnp.einsum('bhqd,bhkd->bhqk', qb, k).astype(jnp.float32) * scale
        q_pos = bi * MLA_Q_BLOCK + jnp.arange(MLA_Q_BLOCK)
        s = jnp.where(k_pos[None, :] <= q_pos[:, None], s, -jnp.inf)
        p = jax.nn.softmax(s, axis=-1).astype(v.dtype)
        return jnp.einsum('bhqk,bhkd->bhqd', p, v)

    out = lax.map(block, jnp.arange(S // MLA_Q_BLOCK))
    return out.transpose(1, 0, 3, 2, 4).reshape(B, S, H * dv)


def mla_attention(c_q, c_kv, k_pe, q_norm_g, w_q_up, kv_norm_g, w_kv_up, cos_b, sin_b):
    B, S, _ = c_q.shape
    q = (rms_norm(c_q, q_norm_g) @ w_q_up).reshape(B, S, B_HEADS, B_NOPE + B_ROPE)
    q_nope, q_pe = q[..., :B_NOPE], q[..., B_NOPE:]
    q_pe = apply_rope(q_pe, cos_b, sin_b)
    kv = (rms_norm(c_kv, kv_norm_g) @ w_kv_up).reshape(B, S, B_HEADS, B_NOPE + B_V)
    k_nope, v = kv[..., :B_NOPE], kv[..., B_NOPE:]
    k_pe = apply_rope(k_pe[:, :, None, :], cos_b, sin_b)
    k_pe = jnp.broadcast_to(k_pe, (B, S, B_HEADS, B_ROPE))
    q_full = jnp.concatenate([q_nope, q_pe], axis=-1).transpose(0, 2, 1, 3)
    k_full = jnp.concatenate([k_nope, k_pe], axis=-1).transpose(0, 2, 1, 3)
    v = v.transpose(0, 2, 1, 3)
    return causal_attention_blocks(q_full, k_full, v, 1.0 / math.sqrt(B_NOPE + B_ROPE))


def peer_ffn(h, w_query, sub_keys_1, sub_keys_2, expert_u, expert_v):
    B, S, D = h.shape
    T = B * S
    hc = h.reshape(T // PEER_TOKEN_CHUNK, PEER_TOKEN_CHUNK, D)

    def chunk(xc):
        q = (xc @ w_query).reshape(PEER_TOKEN_CHUNK, PEER_HEADS, 2, PEER_HALF)
        s1 = jnp.einsum('thd,hnd->thn', q[:, :, 0], sub_keys_1).astype(jnp.float32)
        s2 = jnp.einsum('thd,hnd->thn', q[:, :, 1], sub_keys_2).astype(jnp.float32)
        v1, i1 = lax.top_k(s1, PEER_TOPK)
        v2, i2 = lax.top_k(s2, PEER_TOPK)
        cand = (v1[..., :, None] + v2[..., None, :]).reshape(PEER_TOKEN_CHUNK, PEER_HEADS, PEER_TOPK * PEER_TOPK)
        cand_idx = (i1[..., :, None] * PEER_NKEYS + i2[..., None, :]).reshape(PEER_TOKEN_CHUNK, PEER_HEADS, PEER_TOPK * PEER_TOPK)
        best, pos = lax.top_k(cand, PEER_TOPK)
        idx = jnp.take_along_axis(cand_idx, pos, axis=-1)
        g = jax.nn.softmax(best, axis=-1)
        u = expert_u[idx]
        act = jax.nn.gelu(jnp.einsum('td,thkd->thk', xc, u).astype(jnp.float32), approximate=False)
        w = (g * act).astype(h.dtype)
        return jnp.einsum('thk,thkd->td', w, expert_v[idx])

    return lax.map(chunk, hc).reshape(B, S, D)


def setup_inputs(seed: int = 0) -> dict:
    key = jax.random.key(seed)
    ks = jax.random.split(key, 20)
    f32 = jnp.float32
    L = DEPTH

    def nrm(k, shape, scale):
        return jax.random.normal(k, shape, f32) * scale

    def gain(k, shape):
        return 1.0 + 0.02 * jax.random.normal(k, shape, f32)

    x = jax.random.normal(ks[0], (BATCH, SEQ, D_MODEL), f32)
    positions = jnp.broadcast_to(jnp.arange(SEQ, dtype=jnp.int32), (BATCH, SEQ))
    return {
        "x": x,
        "positions": positions,
        "mix_norm_g": gain(ks[1], (L, D_MODEL)),
        "w_in": nrm(ks[2], (L, D_MODEL, IN_TOTAL), D_MODEL ** -0.5),
        "q_norm_g": gain(ks[3], (L, Q_LORA)),
        "w_q_up": nrm(ks[4], (L, Q_LORA, B_HEADS * (B_NOPE + B_ROPE)), Q_LORA ** -0.5),
        "kv_norm_g": gain(ks[5], (L, KV_LORA)),
        "w_kv_up": nrm(ks[6], (L, KV_LORA, B_HEADS * (B_NOPE + B_V)), KV_LORA ** -0.5),
        "w_branch_a": nrm(ks[7], (L, A_WIDTH, D_MODEL), A_WIDTH ** -0.5),
        "w_branch_b": nrm(ks[8], (L, B_WIDTH, D_MODEL), B_WIDTH ** -0.5),
        "w_out": nrm(ks[9], (L, D_MODEL, D_MODEL), D_MODEL ** -0.5),
        "ffn_norm_g": gain(ks[10], (L, D_MODEL)),
        "w_peer_query": nrm(ks[11], (L, D_MODEL, PEER_HEADS * PEER_QDIM), D_MODEL ** -0.5),
        "peer_sub_keys_1": nrm(ks[12], (L, PEER_HEADS, PEER_NKEYS, PEER_HALF), PEER_HALF ** -0.5),
        "peer_sub_keys_2": nrm(ks[13], (L, PEER_HEADS, PEER_NKEYS, PEER_HALF), PEER_HALF ** -0.5),
        "peer_expert_u": nrm(ks[14], (L, PEER_EXPERTS, D_MODEL), D_MODEL ** -0.5),
        "peer_expert_v": nrm(ks[15], (L, PEER_EXPERTS, D_MODEL), (PEER_HEADS * PEER_TOPK) ** -0.5),
        "final_norm_g": gain(ks[16], (D_MODEL,)),
    }


def reference(x, positions, mix_norm_g, w_in, q_norm_g, w_q_up, kv_norm_g, w_kv_up,
              w_branch_a, w_branch_b, w_out, ffn_norm_g, w_peer_query, peer_sub_keys_1,
              peer_sub_keys_2, peer_expert_u, peer_expert_v, final_norm_g):
    B, S, _ = x.shape
    cos_a, sin_a = rope_tables(positions, A_HEAD_DIM, x.dtype)
    cos_b, sin_b = rope_tables(positions, B_ROPE, x.dtype)
    splits = [sum(IN_WIDTHS[:i + 1]) for i in range(len(IN_WIDTHS) - 1)]
    for layer in range(DEPTH):
        h = rms_norm(x, mix_norm_g[layer])
        proj = h @ w_in[layer]
        q_a, k_a, v_a, c_q, c_kv, k_pe, gate_a, gate_b = jnp.split(proj, splits, axis=-1)
        q_a = apply_rope(q_a.reshape(B, S, A_HEADS, A_HEAD_DIM), cos_a, sin_a)
        k_a = apply_rope(k_a.reshape(B, S, A_HEADS, A_HEAD_DIM), cos_a, sin_a)
        v_a = v_a.reshape(B, S, A_HEADS, A_HEAD_DIM)
        y_a = moba_attention(q_a, k_a, v_a)
        y_b = mla_attention(c_q, c_kv, k_pe, q_norm_g[layer], w_q_up[layer],
                            kv_norm_g[layer], w_kv_up[layer], cos_b, sin_b)
        merged = (jax.nn.sigmoid(gate_a) * (y_a @ w_branch_a[layer])
                  + jax.nn.sigmoid(gate_b) * (y_b @ w_branch_b[layer]))
        x = x + merged @ w_out[layer]
        x = x + peer_ffn(rms_norm(x, ffn_norm_g[layer]), w_peer_query[layer],
                         peer_sub_keys_1[layer], peer_sub_keys_2[layer],
                         peer_expert_u[layer], peer_expert_v[layer])
    return rms_norm(x, final_norm_g)
```

```python
import functools
import math

import jax
import jax.numpy as jnp
from jax import lax
from jax.experimental import pallas as pl
from jax.experimental.pallas import tpu as pltpu

F32 = jnp.float32
BF16 = jnp.bfloat16

D_MODEL = 1024
A_HEADS = 8
A_HEAD_DIM = 64
A_WIDTH = A_HEADS * A_HEAD_DIM
MOBA_BLOCK = 256
MOBA_TOPK = 3
B_HEADS = 8
B_NOPE = 64
B_ROPE = 32
B_V = 64
Q_LORA = 256
KV_LORA = 128
PEER_HEADS = 8
PEER_NKEYS = 128
PEER_HALF = 128
PEER_TOPK = 16
ROPE_THETA = 10000.0
RMS_EPS = 1e-6

LANES = 128
NEG_BIG = -1e30
VMEM_LIMIT = 56 * 1024 * 1024

TM_PROJ = 256
TQ = MOBA_BLOCK
TL_ROUTE = 256
TT_EXP = 512
TE_EXP = 512


def _dot(a, b):
    return jnp.dot(a, b, preferred_element_type=F32)


def _dot_nt(a, b):
    return lax.dot_general(a, b, (((1,), (1,)), ((), ())), preferred_element_type=F32)


def _rms(x, g):
    return x * lax.rsqrt(jnp.mean(x * x, axis=-1, keepdims=True) + RMS_EPS) * g


def _inproj_kernel(x_ref, g_ref, wqkv_ref, wc_ref, wg_ref, wqup_ref, wkvup_ref, qng_ref,
                   kvng_ref, ca_ref, sa1_ref, sa2_ref, cb_ref, sb1_ref, sb2_ref,
                   qa_ref, ka_ref, va_ref, qb_ref, kvb_ref, kpe_ref, sg_ref):
    h = _rms(x_ref[...], g_ref[...]).astype(BF16)

    qkv = _dot(h, wqkv_ref[...])
    ca, sa1, sa2 = ca_ref[...], sa1_ref[...], sa2_ref[...]

    def rope_a(xg):
        return xg * ca + pltpu.roll(xg, LANES - 32, 1) * sa1 + pltpu.roll(xg, 32, 1) * sa2

    for grp in range(A_WIDTH // LANES):
        lo = grp * LANES
        qa_ref[:, lo:lo + LANES] = (rope_a(qkv[:, lo:lo + LANES]) * 0.125).astype(BF16)
        ka_ref[:, lo:lo + LANES] = rope_a(qkv[:, A_WIDTH + lo:A_WIDTH + lo + LANES]).astype(BF16)
    va_ref[...] = qkv[:, 2 * A_WIDTH:].astype(BF16)

    c = _dot(h, wc_ref[...])
    cb, sb1, sb2 = cb_ref[...], sb1_ref[...], sb2_ref[...]

    def rope_b(xg):
        return xg * cb + pltpu.roll(xg, LANES - 16, 1) * sb1 + pltpu.roll(xg, 16, 1) * sb2

    cq = _rms(c[:, :Q_LORA], qng_ref[...]).astype(BF16)
    qb = _dot(cq, wqup_ref[...]) * (1.0 / math.sqrt(B_NOPE + B_ROPE))
    for hd in range(B_HEADS):
        lo = hd * LANES
        qb_ref[:, lo:lo + LANES] = rope_b(qb[:, lo:lo + LANES]).astype(BF16)
    ckv = _rms(c[:, Q_LORA:Q_LORA + KV_LORA], kvng_ref[...]).astype(BF16)
    kvb_ref[...] = _dot(ckv, wkvup_ref[...]).astype(BF16)
    kpe_ref[...] = rope_b(c[:, Q_LORA + KV_LORA:]).astype(BF16)

    sg_ref[...] = jax.nn.sigmoid(_dot(h, wg_ref[...])).astype(BF16)


def _softmax_step(s, m, l, acc, v):
    m_new = jnp.maximum(m, jnp.max(s, axis=1, keepdims=True))
    alpha = jnp.exp(m - m_new)
    p = jnp.exp(s - m_new)
    l_new = alpha * l + jnp.sum(p, axis=1, keepdims=True)
    acc_new = alpha * acc + _dot(p.astype(BF16), v)
    return m_new, l_new, acc_new


def _moba_kernel(q_ref, k_ref, v_ref, o_ref, kmh_ref, kml_ref):
    i = pl.program_id(2)
    nblk = k_ref.shape[0] // MOBA_BLOCK

    @pl.when(i == 0)
    def _():
        rows = lax.broadcasted_iota(jnp.int32, (LANES, LANES), 0)
        km = jnp.zeros((LANES, LANES), F32)
        for j in range(nblk):
            blk = k_ref[j * MOBA_BLOCK:(j + 1) * MOBA_BLOCK, :].astype(F32)
            km = jnp.where(rows == j, jnp.mean(blk, axis=0, keepdims=True), km)
        hi = km.astype(BF16)
        kmh_ref[...] = hi
        kml_ref[...] = (km - hi.astype(F32)).astype(BF16)

    q = q_ref[...]
    lane = lax.broadcasted_iota(jnp.int32, (TQ, LANES), 1)
    zero = jnp.zeros_like(q)
    q_heads = (jnp.where(lane < A_HEAD_DIM, q, zero), jnp.where(lane >= A_HEAD_DIM, q, zero))

    q_aug = []
    for qh in q_heads:
        gate = _dot_nt(qh, kmh_ref[...]) + _dot_nt(qh, kml_ref[...])
        valid = lane < i
        g = jnp.where(valid, gate, -jnp.inf)
        thr = jnp.max(g, axis=1, keepdims=True)
        for _ in range(MOBA_TOPK - 1):
            g = jnp.where(g >= thr, -jnp.inf, g)
            thr = jnp.max(g, axis=1, keepdims=True)
        keep = (valid & (gate >= thr)) | (lane == i)
        bias = jnp.where(keep, 0.0, NEG_BIG).astype(BF16)
        q_aug.append(jnp.concatenate([qh, bias], axis=1))

    def k_aug(j):
        start = pl.multiple_of(j * MOBA_BLOCK, MOBA_BLOCK)
        kj = k_ref[pl.ds(start, MOBA_BLOCK), :]
        onehot = jnp.where(lane == j, 1.0, 0.0).astype(BF16)
        return jnp.concatenate([kj, onehot], axis=1), v_ref[pl.ds(start, MOBA_BLOCK), :]

    def body(j, carry):
        ka, vj = k_aug(j)
        out = []
        for hh in range(2):
            m, l, acc = carry[hh]
            out.append(_softmax_step(_dot_nt(q_aug[hh], ka), m, l, acc, vj))
        return tuple(out)

    init = tuple((jnp.full((TQ, 1), NEG_BIG, F32), jnp.zeros((TQ, 1), F32),
                  jnp.zeros((TQ, LANES), F32)) for _ in range(2))
    carry = lax.fori_loop(0, i, body, init)

    ka, vj = k_aug(i)
    row = lax.broadcasted_iota(jnp.int32, (TQ, MOBA_BLOCK), 0)
    col = lax.broadcasted_iota(jnp.int32, (TQ, MOBA_BLOCK), 1)
    outs = []
    for hh in range(2):
        m, l, acc = carry[hh]
        s = jnp.where(col <= row, _dot_nt(q_aug[hh], ka), NEG_BIG)
        m, l, acc = _softmax_step(s, m, l, acc, vj)
        outs.append(acc / l)
    o_ref[...] = jnp.where(lane < A_HEAD_DIM, outs[0], outs[1]).astype(o_ref.dtype)


def _mla_kernel(q_ref, kv_ref, kpe_ref, o_ref):
    i = pl.program_id(2)
    lane = lax.broadcasted_iota(jnp.int32, (TQ, LANES), 1)
    qs = (q_ref[:, :LANES], q_ref[:, LANES:])

    def load(j):
        start = pl.multiple_of(j * TQ, TQ)
        kpe = kpe_ref[pl.ds(start, TQ), :]
        kvs = (kv_ref[pl.ds(start, TQ), :LANES], kv_ref[pl.ds(start, TQ), LANES:])
        return tuple((jnp.where(lane < B_NOPE, kv, kpe), kv) for kv in kvs)

    def body(j, carry):
        kvj = load(j)
        out = []
        for hh in range(2):
            m, l, acc = carry[hh]
            out.append(_softmax_step(_dot_nt(qs[hh], kvj[hh][0]), m, l, acc, kvj[hh][1]))
        return tuple(out)

    init = tuple((jnp.full((TQ, 1), NEG_BIG, F32), jnp.zeros((TQ, 1), F32),
                  jnp.zeros((TQ, LANES), F32)) for _ in range(2))
    carry = lax.fori_loop(0, i, body, init)

    kvj = load(i)
    row = lax.broadcasted_iota(jnp.int32, (TQ, TQ), 0)
    col = lax.broadcasted_iota(jnp.int32, (TQ, TQ), 1)
    outs = []
    for hh in range(2):
        m, l, acc = carry[hh]
        s = jnp.where(col <= row, _dot_nt(qs[hh], kvj[hh][0]), NEG_BIG)
        m, l, acc = _softmax_step(s, m, l, acc, kvj[hh][1])
        outs.append(acc / l)
    o_ref[...] = jnp.where(lane < B_V, pltpu.roll(outs[0], B_V, 1), outs[1]).astype(o_ref.dtype)


def _merge_kernel(x_ref, ya_ref, yb_ref, sg_ref, wba_ref, wbb_ref, wout_ref, fg_ref, wpq_ref,
                  sk1_ref, sk2_ref, x1_ref, h2_ref, s1_ref, s2_ref):
    merged = (sg_ref[:, :D_MODEL].astype(F32) * _dot(ya_ref[...], wba_ref[...])
              + sg_ref[:, D_MODEL:].astype(F32) * _dot(yb_ref[...], wbb_ref[...]))
    x1 = x_ref[...] + _dot(merged.astype(BF16), wout_ref[...])
    x1_ref[...] = x1
    h2 = _rms(x1, fg_ref[...]).astype(BF16)
    h2_ref[...] = h2
    qp = _dot(h2, wpq_ref[...]).astype(BF16)
    for hd in range(PEER_HEADS):
        lo = 2 * hd * PEER_HALF
        s1_ref[hd] = _dot_nt(sk1_ref[hd], qp[:, lo:lo + PEER_HALF])
        s2_ref[hd] = _dot_nt(sk2_ref[hd], qp[:, lo + PEER_HALF:lo + 2 * PEER_HALF])


def _route_kernel(s1_ref, s2_ref, a_ref, e2_ref, tau_ref, v1_ref, v2_ref, cand_ref, best_ref):
    def top16(src, dst_ref):
        cur = src
        for r in range(PEER_TOPK):
            mx = jnp.max(cur, axis=0, keepdims=True)
            dst_ref[r:r + 1, :] = mx
            cur = jnp.where(cur == mx, -jnp.inf, cur)

    s1 = s1_ref[...]
    s2 = s2_ref[...]
    top16(s1, v1_ref)
    top16(s2, v2_ref)
    v2 = v2_ref[...]
    for r in range(PEER_TOPK):
        cand_ref[r * PEER_TOPK:(r + 1) * PEER_TOPK, :] = v1_ref[r:r + 1, :] + v2
    top16(cand_ref[...], best_ref)
    best = best_ref[...]
    z = jnp.sum(jnp.exp(best - best[0:1, :]), axis=0, keepdims=True)
    tau_ref[...] = best[PEER_TOPK - 1:PEER_TOPK, :]
    a_ref[...] = jnp.exp(s1 - v1_ref[0:1, :]) / z
    e2_ref[...] = jnp.exp(s2 - v2_ref[0:1, :])


def _expert_kernel(h2_ref, u_ref, vt_ref, s1_ref, s2_ref, a_ref, e2_ref, tau_ref, x1_ref, fg_ref,
                   o_ref, acc_ref):
    e = pl.program_id(1)

    @pl.when(e == 0)
    def _():
        acc_ref[...] = jnp.zeros_like(acc_ref)

    pre = _dot_nt(u_ref[...], h2_ref[...])
    act = 0.5 * pre * (1.0 + lax.erf(pre * math.sqrt(0.5)))
    slabs = TE_EXP // PEER_NKEYS
    pieces = []
    for sl in range(slabs):
        i1 = e * slabs + sl
        w = jnp.zeros((PEER_NKEYS, TT_EXP), F32)
        for hd in range(PEER_HEADS):
            cand = s2_ref[hd] + s1_ref[hd, pl.ds(i1, 1), :]
            w = w + jnp.where(cand >= tau_ref[hd], e2_ref[hd], 0.0) * a_ref[hd, pl.ds(i1, 1), :]
        pieces.append((w * act[sl * PEER_NKEYS:(sl + 1) * PEER_NKEYS, :]).astype(BF16))
    wa = jnp.concatenate(pieces, axis=0)
    acc_ref[...] += _dot(vt_ref[...], wa)

    @pl.when(e == pl.num_programs(1) - 1)
    def _():
        o_ref[...] = _rms(x1_ref[...] + acc_ref[...].T, fg_ref[...])


def _params(*sem):
    return pltpu.CompilerParams(dimension_semantics=sem, vmem_limit_bytes=VMEM_LIMIT)


def _full(shape):
    nd = len(shape)
    return pl.BlockSpec(shape, lambda *_: (0,) * nd)


def _rope_tables(positions):
    pos = positions.reshape(-1).astype(F32)[:, None]

    def cs(dim):
        inv_freq = ROPE_THETA ** (-jnp.arange(0, dim, 2, dtype=F32) / dim)
        ang = pos * inv_freq
        return jnp.cos(ang), jnp.sin(ang)

    ca, sa = cs(A_HEAD_DIM)
    za = jnp.zeros_like(sa)
    cb, sb = cs(B_ROPE)
    t = pos.shape[0]
    zb = lambda n: jnp.zeros((t, n), F32)
    ob = lambda n: jnp.ones((t, n), F32)
    return (jnp.concatenate([ca] * 4, axis=1),
            jnp.concatenate([-sa, za, -sa, za], axis=1),
            jnp.concatenate([za, sa, za, sa], axis=1),
            jnp.concatenate([ob(B_NOPE), cb, cb, ob(32)], axis=1),
            jnp.concatenate([zb(B_NOPE), -sb, zb(16), zb(32)], axis=1),
            jnp.concatenate([zb(B_NOPE), zb(16), sb, zb(32)], axis=1))


def kernel(x, positions, mix_norm_g, w_in, q_norm_g, w_q_up, kv_norm_g, w_kv_up, w_branch_a,
           w_branch_b, w_out, ffn_norm_g, w_peer_query, peer_sub_keys_1, peer_sub_keys_2,
           peer_expert_u, peer_expert_v, final_norm_g):
    bsz, seq, d = x.shape
    t = bsz * seq
    n_exp = peer_expert_u.shape[1]
    assert d == D_MODEL and seq % TQ == 0 and t % TT_EXP == 0 and n_exp % TE_EXP == 0
    assert mix_norm_g.shape[0] == 1, "single-layer block"

    w = w_in[0]
    o_cq = 3 * A_WIDTH
    o_ckv = o_cq + Q_LORA
    o_kpe = o_ckv + KV_LORA
    o_g = o_kpe + B_ROPE
    wqkv = w[:, :o_cq].astype(BF16)
    wc = jnp.concatenate([w[:, o_cq:o_kpe], jnp.zeros((d, B_NOPE), F32), w[:, o_kpe:o_g],
                          jnp.zeros((d, LANES - B_NOPE - B_ROPE), F32)], axis=1).astype(BF16)
    wg = w[:, o_g:].astype(BF16)
    wqup = jnp.pad(w_q_up[0].reshape(Q_LORA, B_HEADS, B_NOPE + B_ROPE),
                   ((0, 0), (0, 0), (0, LANES - B_NOPE - B_ROPE))).reshape(Q_LORA, B_HEADS * LANES)
    wqup = wqup.astype(BF16)
    wkvup = w_kv_up[0].astype(BF16)
    row = lambda v: v.reshape(1, -1).astype(F32)
    tables = _rope_tables(positions)
    x2d = x.reshape(t, d)

    tm = TM_PROJ
    tok = lambda width: pl.BlockSpec((tm, width), lambda i: (i, 0))
    bf = lambda width: jax.ShapeDtypeStruct((t, width), BF16)
    qa, ka, va, qb, kvb, kpe, sg = pl.pallas_call(
        _inproj_kernel,
        grid=(t // tm,),
        in_specs=[tok(d), _full((1, d)), _full(wqkv.shape), _full(wc.shape), _full(wg.shape),
                  _full(wqup.shape), _full(wkvup.shape), _full((1, Q_LORA)), _full((1, KV_LORA))]
                 + [tok(LANES)] * 6,
        out_specs=[tok(A_WIDTH), tok(A_WIDTH), tok(A_WIDTH), tok(B_HEADS * LANES),
                   tok(B_HEADS * LANES), tok(LANES), tok(2 * d)],
        out_shape=[bf(A_WIDTH), bf(A_WIDTH), bf(A_WIDTH), bf(B_HEADS * LANES),
                   bf(B_HEADS * LANES), bf(LANES), bf(2 * d)],
        compiler_params=_params("parallel"),
        name="inproj",
    )(x2d, row(mix_norm_g[0]), wqkv, wc, wg, wqup, wkvup, row(q_norm_g[0]), row(kv_norm_g[0]),
      *tables)

    r3 = lambda a: a.reshape(bsz, seq, a.shape[-1])
    pairs = A_WIDTH // LANES
    ya = pl.pallas_call(
        _moba_kernel,
        grid=(bsz, pairs, seq // TQ),
        in_specs=[pl.BlockSpec((None, TQ, LANES), lambda b, g, i: (b, i, g)),
                  pl.BlockSpec((None, seq, LANES), lambda b, g, i: (b, 0, g)),
                  pl.BlockSpec((None, seq, LANES), lambda b, g, i: (b, 0, g))],
        out_specs=pl.BlockSpec((None, TQ, LANES), lambda b, g, i: (b, i, g)),
        out_shape=jax.ShapeDtypeStruct((bsz, seq, A_WIDTH), BF16),
        scratch_shapes=[pltpu.VMEM((LANES, LANES), BF16), pltpu.VMEM((LANES, LANES), BF16)],
        compiler_params=_params("parallel", "parallel", "arbitrary"),
        name="moba",
    )(r3(qa), r3(ka), r3(va))

    yb = pl.pallas_call(
        _mla_kernel,
        grid=(bsz, B_HEADS // 2, seq // TQ),
        in_specs=[pl.BlockSpec((None, TQ, 2 * LANES), lambda b, g, i: (b, i, g)),
                  pl.BlockSpec((None, seq, 2 * LANES), lambda b, g, i: (b, 0, g)),
                  pl.BlockSpec((None, seq, LANES), lambda b, g, i: (b, 0, 0))],
        out_specs=pl.BlockSpec((None, TQ, LANES), lambda b, g, i: (b, i, g)),
        out_shape=jax.ShapeDtypeStruct((bsz, seq, B_HEADS * B_V), BF16),
        compiler_params=_params("parallel", "parallel", "arbitrary"),
        name="mla",
    )(r3(qb), r3(kvb), r3(kpe))

    wba = w_branch_a[0].astype(BF16)
    wbb = w_branch_b[0].astype(BF16)
    wout = w_out[0].astype(BF16)
    wpq = w_peer_query[0].astype(BF16)
    sk1 = peer_sub_keys_1[0].astype(BF16)
    sk2 = peer_sub_keys_2[0].astype(BF16)
    score_spec = pl.BlockSpec((PEER_HEADS, PEER_NKEYS, tm), lambda i: (0, 0, i))
    score_shape = jax.ShapeDtypeStruct((PEER_HEADS, PEER_NKEYS, t), F32)
    x1, h2, s1t, s2t = pl.pallas_call(
        _merge_kernel,
        grid=(t // tm,),
        in_specs=[tok(d), tok(A_WIDTH), tok(B_HEADS * B_V), tok(2 * d), _full(wba.shape),
                  _full(wbb.shape), _full(wout.shape), _full((1, d)), _full(wpq.shape),
                  _full(sk1.shape), _full(sk2.shape)],
        out_specs=[tok(d), tok(d), score_spec, score_spec],
        out_shape=[jax.ShapeDtypeStruct((t, d), F32), bf(d), score_shape, score_shape],
        compiler_params=_params("parallel"),
        name="merge",
    )(x2d, ya.reshape(t, A_WIDTH), yb.reshape(t, B_HEADS * B_V), sg, wba, wbb, wout,
      row(ffn_norm_g[0]), wpq, sk1, sk2)

    tl = TL_ROUTE
    hs = pl.BlockSpec((None, PEER_NKEYS, tl), lambda h, i: (h, 0, i))
    a_co, e2, tau = pl.pallas_call(
        _route_kernel,
        grid=(PEER_HEADS, t // tl),
        in_specs=[hs, hs],
        out_specs=[hs, hs, pl.BlockSpec((None, 1, tl), lambda h, i: (h, 0, i))],
        out_shape=[score_shape, score_shape, jax.ShapeDtypeStruct((PEER_HEADS, 1, t), F32)],
        scratch_shapes=[pltpu.VMEM((PEER_TOPK, tl), F32), pltpu.VMEM((PEER_TOPK, tl), F32),
                        pltpu.VMEM((PEER_TOPK * PEER_TOPK, tl), F32),
                        pltpu.VMEM((PEER_TOPK, tl), F32)],
        compiler_params=_params("parallel", "parallel"),
        name="route",
    )(s1t, s2t)

    u_bf = peer_expert_u[0].astype(BF16)
    vt_bf = peer_expert_v[0].T.astype(BF16)
    tt, te = TT_EXP, TE_EXP
    rt = pl.BlockSpec((PEER_HEADS, PEER_NKEYS, tt), lambda i, e: (0, 0, i))
    out = pl.pallas_call(
        _expert_kernel,
        grid=(t // tt, n_exp // te),
        in_specs=[pl.BlockSpec((tt, d), lambda i, e: (i, 0)),
                  pl.BlockSpec((te, d), lambda i, e: (e, 0)),
                  pl.BlockSpec((d, te), lambda i, e: (0, e)),
                  rt, rt, rt, rt,
                  pl.BlockSpec((PEER_HEADS, 1, tt), lambda i, e: (0, 0, i)),
                  pl.BlockSpec((tt, d), lambda i, e: (i, 0)),
                  pl.BlockSpec((1, d), lambda i, e: (0, 0))],
        out_specs=pl.BlockSpec((tt, d), lambda i, e: (i, 0)),
        out_shape=jax.ShapeDtypeStruct((t, d), F32),
        scratch_shapes=[pltpu.VMEM((d, tt), F32)],
        compiler_params=_params("parallel", "arbitrary"),
        name="experts",
    )(h2, u_bf, vt_bf, s1t, s2t, a_co, e2, tau, x1, row(final_norm_g))
    return out.reshape(bsz, seq, d)
```

```python
import math

import jax
import jax.numpy as jnp
from jax import lax
from jax.experimental import pallas as pl
from jax.experimental.pallas import tpu as pltpu

F32 = jnp.float32
BF16 = jnp.bfloat16

D_MODEL = 1024
A_HEADS = 8
A_HEAD_DIM = 64
A_WIDTH = A_HEADS * A_HEAD_DIM
MOBA_BLOCK = 256
MOBA_TOPK = 3
B_HEADS = 8
B_NOPE = 64
B_ROPE = 32
B_V = 64
Q_LORA = 256
KV_LORA = 128
PEER_HEADS = 8
PEER_NKEYS = 128
PEER_HALF = 128
PEER_TOPK = 16
ROPE_THETA = 10000.0
RMS_EPS = 1e-6

LANES = 128
NEG_BIG = -1e30
VMEM_LIMIT = 56 * 1024 * 1024

TM_PROJ = 256
TK = MOBA_BLOCK
TQA = 2 * TK
TL_ROUTE = 256
TT_EXP = 512
TE_EXP = 512
TB_EXP = 256
TJ_EXP = 16


def _dot(a, b):
    return jnp.dot(a, b, preferred_element_type=F32)


def _dot_nt(a, b):
    return lax.dot_general(a, b, (((1,), (1,)), ((), ())), preferred_element_type=F32)


def _rms(x, g):
    return x * lax.rsqrt(jnp.mean(x * x, axis=-1, keepdims=True) + RMS_EPS) * g


def _inproj_kernel(x_ref, g_ref, wqkv_ref, wc_ref, wg_ref, wqup_ref, wkvup_ref, qng_ref,
                   kvng_ref, ca_ref, sa1_ref, sa2_ref, cb_ref, sb1_ref, sb2_ref,
                   qa_ref, ka_ref, vat_ref, qb_ref, kvb_ref, kvbt_ref, kpe_ref, sg_ref):
    h = _rms(x_ref[...], g_ref[...]).astype(BF16)

    qkv = _dot(h, wqkv_ref[...])
    ca, sa1, sa2 = ca_ref[...], sa1_ref[...], sa2_ref[...]

    def rope_a(xg):
        return xg * ca + pltpu.roll(xg, LANES - 32, 1) * sa1 + pltpu.roll(xg, 32, 1) * sa2

    for grp in range(A_WIDTH // LANES):
        lo = grp * LANES
        qa_ref[:, lo:lo + LANES] = (rope_a(qkv[:, lo:lo + LANES]) * 0.125).astype(BF16)
        ka_ref[:, lo:lo + LANES] = rope_a(qkv[:, A_WIDTH + lo:A_WIDTH + lo + LANES]).astype(BF16)
    vat_ref[...] = qkv[:, 2 * A_WIDTH:].T.astype(BF16)

    c = _dot(h, wc_ref[...])
    cb, sb1, sb2 = cb_ref[...], sb1_ref[...], sb2_ref[...]

    def rope_b(xg):
        return xg * cb + pltpu.roll(xg, LANES - 16, 1) * sb1 + pltpu.roll(xg, 16, 1) * sb2

    cq = _rms(c[:, :Q_LORA], qng_ref[...]).astype(BF16)
    qb = _dot(cq, wqup_ref[...]) * (1.0 / math.sqrt(B_NOPE + B_ROPE))
    for hd in range(B_HEADS):
        lo = hd * LANES
        qb_ref[:, lo:lo + LANES] = rope_b(qb[:, lo:lo + LANES]).astype(BF16)
    ckv = _rms(c[:, Q_LORA:Q_LORA + KV_LORA], kvng_ref[...]).astype(BF16)
    kv = _dot(ckv, wkvup_ref[...])
    kvb_ref[...] = kv.astype(BF16)
    kvbt_ref[...] = kv.T.astype(BF16)
    kpe_ref[...] = rope_b(c[:, Q_LORA + KV_LORA:]).astype(BF16)

    sg_ref[...] = jax.nn.sigmoid(_dot(h, wg_ref[...])).astype(BF16)


def _attn_step(scores, carry, acc_ref, vts):
    new, ps, alphas = [], [], []
    for s, (m, l) in zip(scores, carry):
        m_new = jnp.maximum(m, jnp.max(s, axis=0, keepdims=True))
        alpha = jnp.exp(m - m_new)
        p = jnp.exp(s - m_new)
        new.append((m_new, alpha * l + jnp.sum(p, axis=0, keepdims=True)))
        ps.append(p.astype(BF16))
        alphas.append(alpha)
    for hh in range(2):
        acc_ref[hh] = alphas[hh] * acc_ref[hh] + _dot(vts[hh], ps[hh])
    return tuple(new)


def _attn_loop(i, step):
    per = TQA // TK
    init = tuple((jnp.full((1, TQA), NEG_BIG, F32), jnp.zeros((1, TQA), F32)) for _ in range(2))
    carry = lax.fori_loop(0, per * i, lambda j, c: step(j, c, None), init)
    krow = lax.broadcasted_iota(jnp.int32, (TK, TQA), 0)
    qcol = lax.broadcasted_iota(jnp.int32, (TK, TQA), 1)
    for a in range(per):
        j = per * i + a
        carry = step(j, carry, (j * TK + krow) <= (i * TQA + qcol))
    return carry


def _transpose_bf16(x):
    return x.astype(F32).T.astype(BF16)


def _moba_kernel(q_ref, k_ref, vt_ref, o_ref, kmh_ref, kml_ref, acc_ref):
    i = pl.program_id(2)
    nblk = k_ref.shape[0] // TK

    @pl.when(i == 0)
    def _():
        rows = lax.broadcasted_iota(jnp.int32, (LANES, LANES), 0)
        km = jnp.zeros((LANES, LANES), F32)
        for j in range(nblk):
            blk = k_ref[j * TK:(j + 1) * TK, :].astype(F32)
            km = jnp.where(rows == j, jnp.mean(blk, axis=0, keepdims=True), km)
        hi = km.astype(BF16)
        kmh_ref[...] = hi
        kml_ref[...] = (km - hi.astype(F32)).astype(BF16)

    qt = _transpose_bf16(q_ref[...])
    sub = lax.broadcasted_iota(jnp.int32, (LANES, TQA), 0)
    col = lax.broadcasted_iota(jnp.int32, (LANES, TQA), 1)
    own = (TQA // TK) * i + col // TK
    zero = jnp.zeros_like(qt)
    qt_heads = (jnp.where(sub < A_HEAD_DIM, qt, zero), jnp.where(sub >= A_HEAD_DIM, qt, zero))

    q_aug = []
    for qh in qt_heads:
        gate = _dot(kmh_ref[...], qh) + _dot(kml_ref[...], qh)
        valid = sub < own
        g = jnp.where(valid, gate, -jnp.inf)[:nblk]
        thr = jnp.max(g, axis=0, keepdims=True)
        for _ in range(MOBA_TOPK - 1):
            g = jnp.where(g >= thr, -jnp.inf, g)
            thr = jnp.max(g, axis=0, keepdims=True)
        keep = (valid & (gate >= thr)) | (sub == own)
        bias = jnp.where(keep, 0.0, NEG_BIG).astype(BF16)
        q_aug.append(jnp.concatenate([qh, bias], axis=0))

    lane = lax.broadcasted_iota(jnp.int32, (TK, LANES), 1)
    acc_ref[...] = jnp.zeros_like(acc_ref)

    def step(j, carry, causal):
        start = pl.multiple_of(j * TK, TK)
        onehot = jnp.where(lane == j, 1.0, 0.0).astype(BF16)
        ka = jnp.concatenate([k_ref[pl.ds(start, TK), :], onehot], axis=1)
        scores = [_dot(ka, qa) for qa in q_aug]
        if causal is not None:
            scores = [jnp.where(causal, s, NEG_BIG) for s in scores]
        vt = vt_ref[j]
        return _attn_step(scores, carry, acc_ref, (vt, vt))

    carry = _attn_loop(i, step)
    outs = [acc_ref[hh] / carry[hh][1] for hh in range(2)]
    o_ref[...] = jnp.where(sub < A_HEAD_DIM, outs[0], outs[1]).T.astype(o_ref.dtype)


def _mla_kernel(q_ref, kv_ref, kpe_ref, kvt_ref, o_ref, acc_ref):
    i = pl.program_id(2)
    lane = lax.broadcasted_iota(jnp.int32, (TK, LANES), 1)
    qts = (_transpose_bf16(q_ref[:, :LANES]), _transpose_bf16(q_ref[:, LANES:]))
    acc_ref[...] = jnp.zeros_like(acc_ref)

    def step(j, carry, causal):
        start = pl.multiple_of(j * TK, TK)
        kpe = kpe_ref[pl.ds(start, TK), :]
        scores = []
        for hh in range(2):
            kv = kv_ref[pl.ds(start, TK), hh * LANES:(hh + 1) * LANES]
            key = jnp.where(lane < B_NOPE, kv, kpe)
            scores.append(_dot(key, qts[hh]))
        if causal is not None:
            scores = [jnp.where(causal, s, NEG_BIG) for s in scores]
        return _attn_step(scores, carry, acc_ref, (kvt_ref[j, :LANES, :], kvt_ref[j, LANES:, :]))

    carry = _attn_loop(i, step)
    outs = [(acc_ref[hh] / carry[hh][1])[B_NOPE:, :] for hh in range(2)]
    o_ref[...] = jnp.concatenate(outs, axis=0).T.astype(o_ref.dtype)


def _merge_kernel(x_ref, ya_ref, yb_ref, sg_ref, wba_ref, wbb_ref, wout_ref, fg_ref, wpq_ref,
                  sk1_ref, sk2_ref, x1_ref, h2_ref, s1_ref, s2_ref):
    merged = (sg_ref[:, :D_MODEL].astype(F32) * _dot(ya_ref[...], wba_ref[...])
              + sg_ref[:, D_MODEL:].astype(F32) * _dot(yb_ref[...], wbb_ref[...]))
    x1 = x_ref[...] + _dot(merged.astype(BF16), wout_ref[...])
    x1_ref[...] = x1
    h2 = _rms(x1, fg_ref[...]).astype(BF16)
    h2_ref[...] = h2
    qp = _dot(h2, wpq_ref[...]).astype(BF16)
    for hd in range(PEER_HEADS):
        lo = 2 * hd * PEER_HALF
        s1_ref[hd] = _dot_nt(sk1_ref[hd], qp[:, lo:lo + PEER_HALF])
        s2_ref[hd] = _dot_nt(sk2_ref[hd], qp[:, lo + PEER_HALF:lo + 2 * PEER_HALF])


def _route_kernel(s1_ref, s2_ref, a_ref, e2_ref, tau_ref, v1_ref, v2_ref, best_ref):
    def top16(src, dst_ref):
        cur = src
        for r in range(PEER_TOPK):
            mx = jnp.max(cur, axis=0, keepdims=True)
            dst_ref[r:r + 1, :] = mx
            cur = jnp.where(cur == mx, -jnp.inf, cur)

    s1 = s1_ref[...]
    s2 = s2_ref[...]
    top16(s1, v1_ref)
    top16(s2, v2_ref)
    v1 = v1_ref[...]
    v2 = v2_ref[...]
    half = PEER_TOPK // 2
    cand = jnp.concatenate(
        [v1 + v2[0:1]] + [v1[:half] + v2[r:r + 1] for r in range(1, half)] + [v1[0:1] + v2[half:]],
        axis=0)
    top16(cand, best_ref)
    best = best_ref[...]
    z = jnp.sum(jnp.exp(best - best[0:1, :]), axis=0, keepdims=True)
    tau_ref[...] = best[PEER_TOPK - 1:PEER_TOPK, :]
    a_ref[...] = jnp.exp(s1 - v1_ref[0:1, :]) / z
    e2_ref[...] = jnp.exp(s2 - v2_ref[0:1, :])


def _expert_kernel(h2_ref, u_ref, vt_ref, s1_ref, s2_ref, a_ref, e2_ref, tau_ref, x1_ref, fg_ref,
                   o_ref, acc_ref, act0_ref, act1_ref, wa_ref):
    e = pl.program_id(1)
    n_tiles = pl.num_programs(1) - 1
    slabs = TE_EXP // PEER_NKEYS
    prev = jnp.maximum(e - 1, 0)

    @pl.when(e == 0)
    def _():
        acc_ref[...] = jnp.zeros_like(acc_ref)
        act1_ref[...] = jnp.zeros_like(act1_ref)

    def body(act_w_ref, act_r_ref):
        for tb in range(TT_EXP // TB_EXP):
            ts = slice(tb * TB_EXP, (tb + 1) * TB_EXP)
            pre = _dot_nt(u_ref[...], h2_ref[ts, :])
            act_w_ref[:, ts] = 0.5 * pre * (1.0 + lax.erf(pre * math.sqrt(0.5)))
            tau = [tau_ref[hd, :, ts] for hd in range(PEER_HEADS)]
            for sl in range(slabs):
                i1 = prev * slabs + sl
                s1r = [s1_ref[hd, pl.ds(i1, 1), ts] for hd in range(PEER_HEADS)]
                ar = [a_ref[hd, pl.ds(i1, 1), ts] for hd in range(PEER_HEADS)]
                for jt in range(PEER_NKEYS // TJ_EXP):
                    rj = slice(jt * TJ_EXP, (jt + 1) * TJ_EXP)
                    w = None
                    for hd in range(PEER_HEADS):
                        cand = s2_ref[hd, rj, ts] + s1r[hd]
                        term = jnp.where(cand >= tau[hd], e2_ref[hd, rj, ts], 0.0) * ar[hd]
                        w = term if w is None else w + term
                    ro = slice(sl * PEER_NKEYS + jt * TJ_EXP, sl * PEER_NKEYS + (jt + 1) * TJ_EXP)
                    wa_ref[ro, ts] = (w * act_r_ref[ro, ts]).astype(BF16)
            acc_ref[:, ts] += _dot(vt_ref[...], wa_ref[:, ts])

    @pl.when(e % 2 == 0)
    def _():
        body(act0_ref, act1_ref)

    @pl.when(e % 2 == 1)
    def _():
        body(act1_ref, act0_ref)

    @pl.when(e == n_tiles)
    def _():
        o_ref[...] = _rms(x1_ref[...] + acc_ref[...].T, fg_ref[...])


def _params(*sem):
    return pltpu.CompilerParams(dimension_semantics=sem, vmem_limit_bytes=VMEM_LIMIT)


def _full(shape):
    nd = len(shape)
    return pl.BlockSpec(shape, lambda *_: (0,) * nd)


def _rope_tables(positions):
    pos = positions.reshape(-1).astype(F32)[:, None]

    def cs(dim):
        inv_freq = ROPE_THETA ** (-jnp.arange(0, dim, 2, dtype=F32) / dim)
        ang = pos * inv_freq
        return jnp.cos(ang), jnp.sin(ang)

    ca, sa = cs(A_HEAD_DIM)
    za = jnp.zeros_like(sa)
    cb, sb = cs(B_ROPE)
    t = pos.shape[0]
    zb = lambda n: jnp.zeros((t, n), F32)
    ob = lambda n: jnp.ones((t, n), F32)
    return (jnp.concatenate([ca] * 4, axis=1),
            jnp.concatenate([-sa, za, -sa, za], axis=1),
            jnp.concatenate([za, sa, za, sa], axis=1),
            jnp.concatenate([ob(B_NOPE), cb, cb, ob(32)], axis=1),
            jnp.concatenate([zb(B_NOPE), -sb, zb(16), zb(32)], axis=1),
            jnp.concatenate([zb(B_NOPE), zb(16), sb, zb(32)], axis=1))


def kernel(x, positions, mix_norm_g, w_in, q_norm_g, w_q_up, kv_norm_g, w_kv_up, w_branch_a,
           w_branch_b, w_out, ffn_norm_g, w_peer_query, peer_sub_keys_1, peer_sub_keys_2,
           peer_expert_u, peer_expert_v, final_norm_g):
    bsz, seq, d = x.shape
    t = bsz * seq
    n_exp = peer_expert_u.shape[1]
    assert d == D_MODEL and seq % TQA == 0 and t % TT_EXP == 0 and n_exp % TE_EXP == 0
    assert mix_norm_g.shape[0] == 1, "single-layer block"

    w = w_in[0]
    o_cq = 3 * A_WIDTH
    o_ckv = o_cq + Q_LORA
    o_kpe = o_ckv + KV_LORA
    o_g = o_kpe + B_ROPE
    wqkv = w[:, :o_cq].astype(BF16)
    wc = jnp.concatenate([w[:, o_cq:o_kpe], jnp.zeros((d, B_NOPE), F32), w[:, o_kpe:o_g],
                          jnp.zeros((d, LANES - B_NOPE - B_ROPE), F32)], axis=1).astype(BF16)
    wg = w[:, o_g:].astype(BF16)
    wqup = jnp.pad(w_q_up[0].reshape(Q_LORA, B_HEADS, B_NOPE + B_ROPE),
                   ((0, 0), (0, 0), (0, LANES - B_NOPE - B_ROPE))).reshape(Q_LORA, B_HEADS * LANES)
    wqup = wqup.astype(BF16)
    wkvup = w_kv_up[0].astype(BF16)
    row = lambda v: v.reshape(1, -1).astype(F32)
    tables = _rope_tables(positions)
    x2d = x.reshape(t, d)

    tm = TM_PROJ
    assert tm == TK, "one projection tile is one attention key block (transposed value tiles)"
    tok = lambda width: pl.BlockSpec((tm, width), lambda i: (i, 0))
    bf = lambda width: jax.ShapeDtypeStruct((t, width), BF16)
    tok_t = lambda width: pl.BlockSpec((None, width, tm), lambda i: (i, 0, 0))
    bf_t = lambda width: jax.ShapeDtypeStruct((t // tm, width, tm), BF16)
    qa, ka, vat, qb, kvb, kvbt, kpe, sg = pl.pallas_call(
        _inproj_kernel,
        grid=(t // tm,),
        in_specs=[tok(d), _full((1, d)), _full(wqkv.shape), _full(wc.shape), _full(wg.shape),
                  _full(wqup.shape), _full(wkvup.shape), _full((1, Q_LORA)), _full((1, KV_LORA))]
                 + [tok(LANES)] * 6,
        out_specs=[tok(A_WIDTH), tok(A_WIDTH), tok_t(A_WIDTH), tok(B_HEADS * LANES),
                   tok(B_HEADS * LANES), tok_t(B_HEADS * LANES), tok(LANES), tok(2 * d)],
        out_shape=[bf(A_WIDTH), bf(A_WIDTH), bf_t(A_WIDTH), bf(B_HEADS * LANES),
                   bf(B_HEADS * LANES), bf_t(B_HEADS * LANES), bf(LANES), bf(2 * d)],
        compiler_params=_params("parallel"),
        name="inproj",
    )(x2d, row(mix_norm_g[0]), wqkv, wc, wg, wqup, wkvup, row(q_norm_g[0]), row(kv_norm_g[0]),
      *tables)

    r3 = lambda a: a.reshape(bsz, seq, a.shape[-1])
    r4t = lambda a: a.reshape(bsz, seq // tm, a.shape[1], tm)
    nblk = seq // TK
    assert nblk <= LANES
    pairs = A_WIDTH // LANES
    ya = pl.pallas_call(
        _moba_kernel,
        grid=(bsz, pairs, seq // TQA),
        in_specs=[pl.BlockSpec((None, TQA, LANES), lambda b, g, i: (b, i, g)),
                  pl.BlockSpec((None, seq, LANES), lambda b, g, i: (b, 0, g)),
                  pl.BlockSpec((None, nblk, LANES, TK), lambda b, g, i: (b, 0, g, 0))],
        out_specs=pl.BlockSpec((None, TQA, LANES), lambda b, g, i: (b, i, g)),
        out_shape=jax.ShapeDtypeStruct((bsz, seq, A_WIDTH), BF16),
        scratch_shapes=[pltpu.VMEM((LANES, LANES), BF16), pltpu.VMEM((LANES, LANES), BF16),
                        pltpu.VMEM((2, LANES, TQA), F32)],
        compiler_params=_params("parallel", "parallel", "arbitrary"),
        name="moba",
    )(r3(qa), r3(ka), r4t(vat))

    yb = pl.pallas_call(
        _mla_kernel,
        grid=(bsz, B_HEADS // 2, seq // TQA),
        in_specs=[pl.BlockSpec((None, TQA, 2 * LANES), lambda b, g, i: (b, i, g)),
                  pl.BlockSpec((None, seq, 2 * LANES), lambda b, g, i: (b, 0, g)),
                  pl.BlockSpec((None, seq, LANES), lambda b, g, i: (b, 0, 0)),
                  pl.BlockSpec((None, nblk, 2 * LANES, TK), lambda b, g, i: (b, 0, g, 0))],
        out_specs=pl.BlockSpec((None, TQA, LANES), lambda b, g, i: (b, i, g)),
        out_shape=jax.ShapeDtypeStruct((bsz, seq, B_HEADS * B_V), BF16),
        scratch_shapes=[pltpu.VMEM((2, LANES, TQA), F32)],
        compiler_params=_params("parallel", "parallel", "arbitrary"),
        name="mla",
    )(r3(qb), r3(kvb), r3(kpe), r4t(kvbt))

    wba = w_branch_a[0].astype(BF16)
    wbb = w_branch_b[0].astype(BF16)
    wout = w_out[0].astype(BF16)
    wpq = w_peer_query[0].astype(BF16)
    sk1 = peer_sub_keys_1[0].astype(BF16)
    sk2 = peer_sub_keys_2[0].astype(BF16)
    score_spec = pl.BlockSpec((PEER_HEADS, PEER_NKEYS, tm), lambda i: (0, 0, i))
    score_shape = jax.ShapeDtypeStruct((PEER_HEADS, PEER_NKEYS, t), F32)
    x1, h2, s1t, s2t = pl.pallas_call(
        _merge_kernel,
        grid=(t // tm,),
        in_specs=[tok(d), tok(A_WIDTH), tok(B_HEADS * B_V), tok(2 * d), _full(wba.shape),
                  _full(wbb.shape), _full(wout.shape), _full((1, d)), _full(wpq.shape),
                  _full(sk1.shape), _full(sk2.shape)],
        out_specs=[tok(d), tok(d), score_spec, score_spec],
        out_shape=[jax.ShapeDtypeStruct((t, d), F32), bf(d), score_shape, score_shape],
        compiler_params=_params("parallel"),
        name="merge",
    )(x2d, ya.reshape(t, A_WIDTH), yb.reshape(t, B_HEADS * B_V), sg, wba, wbb, wout,
      row(ffn_norm_g[0]), wpq, sk1, sk2)

    tl = TL_ROUTE
    hs = pl.BlockSpec((None, PEER_NKEYS, tl), lambda h, i: (h, 0, i))
    a_co, e2, tau = pl.pallas_call(
        _route_kernel,
        grid=(PEER_HEADS, t // tl),
        in_specs=[hs, hs],
        out_specs=[hs, hs, pl.BlockSpec((None, 1, tl), lambda h, i: (h, 0, i))],
        out_shape=[score_shape, score_shape, jax.ShapeDtypeStruct((PEER_HEADS, 1, t), F32)],
        scratch_shapes=[pltpu.VMEM((PEER_TOPK, tl), F32)] * 3,
        compiler_params=_params("parallel", "parallel"),
        name="route",
    )(s1t, s2t)

    u_bf = peer_expert_u[0].astype(BF16)
    vt_bf = peer_expert_v[0].T.astype(BF16)
    tt, te = TT_EXP, TE_EXP
    rt = pl.BlockSpec((PEER_HEADS, PEER_NKEYS, tt), lambda i, e: (0, 0, i))
    n_tiles = n_exp // te
    out = pl.pallas_call(
        _expert_kernel,
        grid=(t // tt, n_tiles + 1),
        in_specs=[pl.BlockSpec((tt, d), lambda i, e: (i, 0)),
                  pl.BlockSpec((te, d), lambda i, e: (jnp.minimum(e, n_tiles - 1), 0)),
                  pl.BlockSpec((d, te), lambda i, e: (0, jnp.maximum(e - 1, 0))),
                  rt, rt, rt, rt,
                  pl.BlockSpec((PEER_HEADS, 1, tt), lambda i, e: (0, 0, i)),
                  pl.BlockSpec((tt, d), lambda i, e: (i, 0)),
                  pl.BlockSpec((1, d), lambda i, e: (0, 0))],
        out_specs=pl.BlockSpec((tt, d), lambda i, e: (i, 0)),
        out_shape=jax.ShapeDtypeStruct((t, d), F32),
        scratch_shapes=[pltpu.VMEM((d, tt), F32), pltpu.VMEM((te, tt), F32),
                        pltpu.VMEM((te, tt), F32), pltpu.VMEM((te, tt), BF16)],
        compiler_params=_params("parallel", "arbitrary"),
        name="experts",
    )(h2, u_bf, vt_bf, s1t, s2t, a_co, e2, tau, x1, row(final_norm_g))
    return out.reshape(bsz, seq, d)
```

```python
import math

import jax
import jax.numpy as jnp
from jax import lax
from jax.experimental import pallas as pl
from jax.experimental.pallas import tpu as pltpu

F32 = jnp.float32
BF16 = jnp.bfloat16

D_MODEL = 1024
A_HEADS = 8
A_HEAD_DIM = 64
A_WIDTH = A_HEADS * A_HEAD_DIM
MOBA_BLOCK = 256
MOBA_TOPK = 3
B_HEADS = 8
B_NOPE = 64
B_ROPE = 32
B_V = 64
Q_LORA = 256
KV_LORA = 128
PEER_HEADS = 8
PEER_NKEYS = 128
PEER_HALF = 128
PEER_TOPK = 16
ROPE_THETA = 10000.0
RMS_EPS = 1e-6

LANES = 128
NEG_BIG = -1e30
LOG2E = math.log2(math.e)
VMEM_LIMIT = 56 * 1024 * 1024

TM_PROJ = 256
TK = MOBA_BLOCK
TQA = 2 * TK
TL_ROUTE = 256
TT_EXP = 512
TE_EXP = 1024
TB_EXP = 256
TJ_EXP = 16


def _dot(a, b):
    return jnp.dot(a, b, preferred_element_type=F32)


def _dot_nt(a, b):
    return lax.dot_general(a, b, (((1,), (1,)), ((), ())), preferred_element_type=F32)


def _rms(x, g):
    return x * lax.rsqrt(jnp.mean(x * x, axis=-1, keepdims=True) + RMS_EPS) * g


def _inproj_kernel(x_ref, g_ref, wqkv_ref, wc_ref, wg_ref, wqup_ref, wkvup_ref, qng_ref,
                   kvng_ref, ca_ref, sa1_ref, sa2_ref, cb_ref, sb1_ref, sb2_ref,
                   qa_ref, ka_ref, vat_ref, qb_ref, kvb_ref, kvbt_ref, kpe_ref, sg_ref):
    h = _rms(x_ref[...], g_ref[...]).astype(BF16)

    qkv = _dot(h, wqkv_ref[...])
    ca, sa1, sa2 = ca_ref[...], sa1_ref[...], sa2_ref[...]

    def rope_a(xg):
        return xg * ca + pltpu.roll(xg, LANES - 32, 1) * sa1 + pltpu.roll(xg, 32, 1) * sa2

    for grp in range(A_WIDTH // LANES):
        lo = grp * LANES
        qa_ref[:, lo:lo + LANES] = (rope_a(qkv[:, lo:lo + LANES]) * (0.125 * LOG2E)).astype(BF16)
        ka_ref[:, lo:lo + LANES] = rope_a(qkv[:, A_WIDTH + lo:A_WIDTH + lo + LANES]).astype(BF16)
    vat_ref[...] = qkv[:, 2 * A_WIDTH:].T.astype(BF16)

    c = _dot(h, wc_ref[...])
    cb, sb1, sb2 = cb_ref[...], sb1_ref[...], sb2_ref[...]

    def rope_b(xg):
        return xg * cb + pltpu.roll(xg, LANES - 16, 1) * sb1 + pltpu.roll(xg, 16, 1) * sb2

    cq = _rms(c[:, :Q_LORA], qng_ref[...]).astype(BF16)
    qb = _dot(cq, wqup_ref[...]) * (LOG2E / math.sqrt(B_NOPE + B_ROPE))
    for hd in range(B_HEADS):
        lo = hd * LANES
        qb_ref[:, lo:lo + LANES] = rope_b(qb[:, lo:lo + LANES]).astype(BF16)
    ckv = _rms(c[:, Q_LORA:Q_LORA + KV_LORA], kvng_ref[...]).astype(BF16)
    kv = _dot(ckv, wkvup_ref[...])
    kvb_ref[...] = kv.astype(BF16)
    kvbt_ref[...] = kv.T.astype(BF16)
    kpe_ref[...] = rope_b(c[:, Q_LORA + KV_LORA:]).astype(BF16)

    sg_ref[...] = jax.nn.sigmoid(_dot(h, wg_ref[...])).astype(BF16)


def _attn_step(scores, carry, acc_ref, vts):
    new, ps, alphas = [], [], []
    for s, (m, l) in zip(scores, carry):
        m_new = jnp.maximum(m, jnp.max(s, axis=0, keepdims=True))
        alpha = jnp.exp2(m - m_new)
        p = jnp.exp2(s - m_new)
        new.append((m_new, alpha * l + jnp.sum(p, axis=0, keepdims=True)))
        ps.append(p.astype(BF16))
        alphas.append(alpha)
    for hh in range(2):
        acc_ref[hh] = alphas[hh] * acc_ref[hh] + _dot(vts[hh], ps[hh])
    return tuple(new)


def _attn_loop(i, step):
    per = TQA // TK
    init = tuple((jnp.full((1, TQA), NEG_BIG, F32), jnp.zeros((1, TQA), F32)) for _ in range(2))
    carry = lax.fori_loop(0, per * i, lambda j, c: step(j, c, None), init)
    krow = lax.broadcasted_iota(jnp.int32, (TK, TQA), 0)
    qcol = lax.broadcasted_iota(jnp.int32, (TK, TQA), 1)
    for a in range(per):
        j = per * i + a
        carry = step(j, carry, (j * TK + krow) <= (i * TQA + qcol))
    return carry


def _transpose_bf16(x):
    return x.astype(F32).T.astype(BF16)


def _moba_kernel(q_ref, k_ref, vt_ref, o_ref, kmh_ref, kml_ref, acc_ref):
    i = pl.program_id(2)
    nblk = k_ref.shape[0] // TK

    @pl.when(i == 0)
    def _():
        rows = lax.broadcasted_iota(jnp.int32, (LANES, LANES), 0)
        km = jnp.zeros((LANES, LANES), F32)
        for j in range(nblk):
            blk = k_ref[j * TK:(j + 1) * TK, :].astype(F32)
            km = jnp.where(rows == j, jnp.mean(blk, axis=0, keepdims=True), km)
        hi = km.astype(BF16)
        kmh_ref[...] = hi
        kml_ref[...] = (km - hi.astype(F32)).astype(BF16)

    qt = _transpose_bf16(q_ref[...])
    sub = lax.broadcasted_iota(jnp.int32, (LANES, TQA), 0)
    col = lax.broadcasted_iota(jnp.int32, (LANES, TQA), 1)
    own = (TQA // TK) * i + col // TK
    zero = jnp.zeros_like(qt)
    qt_heads = (jnp.where(sub < A_HEAD_DIM, qt, zero), jnp.where(sub >= A_HEAD_DIM, qt, zero))

    q_aug = []
    for qh in qt_heads:
        gate = _dot(kmh_ref[...], qh) + _dot(kml_ref[...], qh)
        valid = sub < own
        g = jnp.where(valid, gate, -jnp.inf)[:nblk]
        thr = jnp.max(g, axis=0, keepdims=True)
        for _ in range(MOBA_TOPK - 1):
            g = jnp.where(g >= thr, -jnp.inf, g)
            thr = jnp.max(g, axis=0, keepdims=True)
        keep = (valid & (gate >= thr)) | (sub == own)
        bias = jnp.where(keep, 0.0, NEG_BIG).astype(BF16)
        q_aug.append(jnp.concatenate([qh, bias], axis=0))

    lane = lax.broadcasted_iota(jnp.int32, (TK, LANES), 1)
    acc_ref[...] = jnp.zeros_like(acc_ref)

    def step(j, carry, causal):
        start = pl.multiple_of(j * TK, TK)
        onehot = jnp.where(lane == j, 1.0, 0.0).astype(BF16)
        ka = jnp.concatenate([k_ref[pl.ds(start, TK), :], onehot], axis=1)
        scores = [_dot(ka, qa) for qa in q_aug]
        if causal is not None:
            scores = [jnp.where(causal, s, NEG_BIG) for s in scores]
        vt = vt_ref[j]
        return _attn_step(scores, carry, acc_ref, (vt, vt))

    carry = _attn_loop(i, step)
    outs = [acc_ref[hh] / carry[hh][1] for hh in range(2)]
    o_ref[...] = jnp.where(sub < A_HEAD_DIM, outs[0], outs[1]).T.astype(o_ref.dtype)


def _mla_kernel(q_ref, kv_ref, kpe_ref, kvt_ref, o_ref, acc_ref):
    i = pl.program_id(2)
    lane = lax.broadcasted_iota(jnp.int32, (TK, LANES), 1)
    qts = (_transpose_bf16(q_ref[:, :LANES]), _transpose_bf16(q_ref[:, LANES:]))
    acc_ref[...] = jnp.zeros_like(acc_ref)

    def step(j, carry, causal):
        start = pl.multiple_of(j * TK, TK)
        kpe = kpe_ref[pl.ds(start, TK), :]
        scores = []
        for hh in range(2):
            kv = kv_ref[pl.ds(start, TK), hh * LANES:(hh + 1) * LANES]
            key = jnp.where(lane < B_NOPE, kv, kpe)
            scores.append(_dot(key, qts[hh]))
        if causal is not None:
            scores = [jnp.where(causal, s, NEG_BIG) for s in scores]
        return _attn_step(scores, carry, acc_ref, (kvt_ref[j, :LANES, :], kvt_ref[j, LANES:, :]))

    carry = _attn_loop(i, step)
    outs = [(acc_ref[hh] / carry[hh][1])[B_NOPE:, :] for hh in range(2)]
    o_ref[...] = jnp.concatenate(outs, axis=0).T.astype(o_ref.dtype)


def _merge_kernel(x_ref, ya_ref, yb_ref, sg_ref, wba_ref, wbb_ref, wout_ref, fg_ref, wpq_ref,
                  sk1_ref, sk2_ref, x1_ref, h2_ref, s1_ref, s2_ref):
    merged = (sg_ref[:, :D_MODEL].astype(F32) * _dot(ya_ref[...], wba_ref[...])
              + sg_ref[:, D_MODEL:].astype(F32) * _dot(yb_ref[...], wbb_ref[...]))
    x1 = x_ref[...] + _dot(merged.astype(BF16), wout_ref[...])
    x1_ref[...] = x1
    h2 = _rms(x1, fg_ref[...]).astype(BF16)
    h2_ref[...] = h2
    qp = _dot(h2, wpq_ref[...]).astype(BF16)
    for hd in range(PEER_HEADS):
        lo = 2 * hd * PEER_HALF
        s1_ref[hd] = _dot_nt(sk1_ref[hd], qp[:, lo:lo + PEER_HALF])
        s2_ref[hd] = _dot_nt(sk2_ref[hd], qp[:, lo + PEER_HALF:lo + 2 * PEER_HALF])


def _route_kernel(s1_ref, s2_ref, a_ref, e2_ref, tau_ref, v1_ref, v2_ref, best_ref):
    def top16(src, dst_ref):
        cur = src
        for r in range(PEER_TOPK):
            mx = jnp.max(cur, axis=0, keepdims=True)
            dst_ref[r:r + 1, :] = mx
            cur = jnp.where(cur == mx, -jnp.inf, cur)

    def head(hd, carry):
        s1 = s1_ref[hd]
        s2 = s2_ref[hd]
        top16(s1, v1_ref)
        top16(s2, v2_ref)
        v1 = v1_ref[...]
        v2 = v2_ref[...]
        half = PEER_TOPK // 2
        cand = jnp.concatenate(
            [v1 + v2[0:1]] + [v1[:half] + v2[r:r + 1] for r in range(1, half)]
            + [v1[0:1] + v2[half:]], axis=0)
        top16(cand, best_ref)
        best = best_ref[...]
        z = jnp.sum(jnp.exp(best - best[0:1, :]), axis=0, keepdims=True)
        tau_ref[hd] = best[PEER_TOPK - 1:PEER_TOPK, :]
        a_ref[hd] = jnp.exp(s1 - v1_ref[0:1, :]) / z
        e2_ref[hd] = jnp.exp(s2 - v2_ref[0:1, :])
        return carry

    lax.fori_loop(0, PEER_HEADS, head, 0)


def _expert_kernel(h2_ref, u_ref, vt_ref, s1_ref, s2_ref, a_ref, e2_ref, tau_ref, x1_ref, fg_ref,
                   o_ref, acc_ref, act0_ref, act1_ref, wa_ref):
    e = pl.program_id(1)
    n_tiles = pl.num_programs(1) - 1
    slabs = TE_EXP // PEER_NKEYS
    prev = jnp.maximum(e - 1, 0)

    @pl.when(e == 0)
    def _():
        acc_ref[...] = jnp.zeros_like(acc_ref)
        act1_ref[...] = jnp.zeros_like(act1_ref)

    def body(act_w_ref, act_r_ref):
        for tb in range(TT_EXP // TB_EXP):
            ts = slice(tb * TB_EXP, (tb + 1) * TB_EXP)
            pre = _dot_nt(u_ref[...], h2_ref[ts, :])
            act_w_ref[:, ts] = 0.5 * pre * (1.0 + lax.erf(pre * math.sqrt(0.5)))
            tau = [tau_ref[hd, :, ts] for hd in range(PEER_HEADS)]
            for sl in range(slabs):
                i1 = prev * slabs + sl
                s1r = [s1_ref[hd, pl.ds(i1, 1), ts] for hd in range(PEER_HEADS)]
                ar = [a_ref[hd, pl.ds(i1, 1), ts] for hd in range(PEER_HEADS)]
                for jt in range(PEER_NKEYS // TJ_EXP):
                    rj = slice(jt * TJ_EXP, (jt + 1) * TJ_EXP)
                    w = None
                    for hd in range(PEER_HEADS):
                        cand = s2_ref[hd, rj, ts] + s1r[hd]
                        term = jnp.where(cand >= tau[hd], e2_ref[hd, rj, ts], 0.0) * ar[hd]
                        w = term if w is None else w + term
                    ro = slice(sl * PEER_NKEYS + jt * TJ_EXP, sl * PEER_NKEYS + (jt + 1) * TJ_EXP)
                    wa_ref[ro, ts] = (w * act_r_ref[ro, ts]).astype(BF16)
            acc_ref[:, ts] += _dot(vt_ref[...], wa_ref[:, ts])

    @pl.when(e % 2 == 0)
    def _():
        body(act0_ref, act1_ref)

    @pl.when(e % 2 == 1)
    def _():
        body(act1_ref, act0_ref)

    @pl.when(e == n_tiles)
    def _():
        o_ref[...] = _rms(x1_ref[...] + acc_ref[...].T, fg_ref[...])


def _params(*sem):
    return pltpu.CompilerParams(dimension_semantics=sem, vmem_limit_bytes=VMEM_LIMIT)


def _full(shape):
    nd = len(shape)
    return pl.BlockSpec(shape, lambda *_: (0,) * nd)


def _rope_tables(positions):
    pos = positions.reshape(-1).astype(F32)[:, None]

    def cs(dim):
        inv_freq = ROPE_THETA ** (-jnp.arange(0, dim, 2, dtype=F32) / dim)
        ang = pos * inv_freq
        return jnp.cos(ang), jnp.sin(ang)

    ca, sa = cs(A_HEAD_DIM)
    za = jnp.zeros_like(sa)
    cb, sb = cs(B_ROPE)
    t = pos.shape[0]
    zb = lambda n: jnp.zeros((t, n), F32)
    ob = lambda n: jnp.ones((t, n), F32)
    return (jnp.concatenate([ca] * 4, axis=1),
            jnp.concatenate([-sa, za, -sa, za], axis=1),
            jnp.concatenate([za, sa, za, sa], axis=1),
            jnp.concatenate([ob(B_NOPE), cb, cb, ob(32)], axis=1),
            jnp.concatenate([zb(B_NOPE), -sb, zb(16), zb(32)], axis=1),
            jnp.concatenate([zb(B_NOPE), zb(16), sb, zb(32)], axis=1))


def kernel(x, positions, mix_norm_g, w_in, q_norm_g, w_q_up, kv_norm_g, w_kv_up, w_branch_a,
           w_branch_b, w_out, ffn_norm_g, w_peer_query, peer_sub_keys_1, peer_sub_keys_2,
           peer_expert_u, peer_expert_v, final_norm_g):
    bsz, seq, d = x.shape
    t = bsz * seq
    n_exp = peer_expert_u.shape[1]
    assert d == D_MODEL and seq % TQA == 0 and t % TT_EXP == 0 and n_exp % TE_EXP == 0
    assert mix_norm_g.shape[0] == 1, "single-layer block"

    w = w_in[0]
    o_cq = 3 * A_WIDTH
    o_ckv = o_cq + Q_LORA
    o_kpe = o_ckv + KV_LORA
    o_g = o_kpe + B_ROPE
    wqkv = w[:, :o_cq].astype(BF16)
    wc = jnp.concatenate([w[:, o_cq:o_kpe], jnp.zeros((d, B_NOPE), F32), w[:, o_kpe:o_g],
                          jnp.zeros((d, LANES - B_NOPE - B_ROPE), F32)], axis=1).astype(BF16)
    wg = w[:, o_g:].astype(BF16)
    wqup = jnp.pad(w_q_up[0].reshape(Q_LORA, B_HEADS, B_NOPE + B_ROPE),
                   ((0, 0), (0, 0), (0, LANES - B_NOPE - B_ROPE))).reshape(Q_LORA, B_HEADS * LANES)
    wqup = wqup.astype(BF16)
    wkvup = w_kv_up[0].astype(BF16)
    row = lambda v: v.reshape(1, -1).astype(F32)
    tables = _rope_tables(positions)
    x2d = x.reshape(t, d)

    tm = TM_PROJ
    assert tm == TK, "one projection tile is one attention key block (transposed value tiles)"
    tok = lambda width: pl.BlockSpec((tm, width), lambda i: (i, 0))
    bf = lambda width: jax.ShapeDtypeStruct((t, width), BF16)
    tok_t = lambda width: pl.BlockSpec((None, width, tm), lambda i: (i, 0, 0))
    bf_t = lambda width: jax.ShapeDtypeStruct((t // tm, width, tm), BF16)
    qa, ka, vat, qb, kvb, kvbt, kpe, sg = pl.pallas_call(
        _inproj_kernel,
        grid=(t // tm,),
        in_specs=[tok(d), _full((1, d)), _full(wqkv.shape), _full(wc.shape), _full(wg.shape),
                  _full(wqup.shape), _full(wkvup.shape), _full((1, Q_LORA)), _full((1, KV_LORA))]
                 + [tok(LANES)] * 6,
        out_specs=[tok(A_WIDTH), tok(A_WIDTH), tok_t(A_WIDTH), tok(B_HEADS * LANES),
                   tok(B_HEADS * LANES), tok_t(B_HEADS * LANES), tok(LANES), tok(2 * d)],
        out_shape=[bf(A_WIDTH), bf(A_WIDTH), bf_t(A_WIDTH), bf(B_HEADS * LANES),
                   bf(B_HEADS * LANES), bf_t(B_HEADS * LANES), bf(LANES), bf(2 * d)],
        compiler_params=_params("parallel"),
        name="inproj",
    )(x2d, row(mix_norm_g[0]), wqkv, wc, wg, wqup, wkvup, row(q_norm_g[0]), row(kv_norm_g[0]),
      *tables)

    r3 = lambda a: a.reshape(bsz, seq, a.shape[-1])
    r4t = lambda a: a.reshape(bsz, seq // tm, a.shape[1], tm)
    nblk = seq // TK
    assert nblk <= LANES
    pairs = A_WIDTH // LANES
    ya = pl.pallas_call(
        _moba_kernel,
        grid=(bsz, pairs, seq // TQA),
        in_specs=[pl.BlockSpec((None, TQA, LANES), lambda b, g, i: (b, i, g)),
                  pl.BlockSpec((None, seq, LANES), lambda b, g, i: (b, 0, g)),
                  pl.BlockSpec((None, nblk, LANES, TK), lambda b, g, i: (b, 0, g, 0))],
        out_specs=pl.BlockSpec((None, TQA, LANES), lambda b, g, i: (b, i, g)),
        out_shape=jax.ShapeDtypeStruct((bsz, seq, A_WIDTH), BF16),
        scratch_shapes=[pltpu.VMEM((LANES, LANES), BF16), pltpu.VMEM((LANES, LANES), BF16),
                        pltpu.VMEM((2, LANES, TQA), F32)],
        compiler_params=_params("parallel", "parallel", "arbitrary"),
        name="moba",
    )(r3(qa), r3(ka), r4t(vat))

    yb = pl.pallas_call(
        _mla_kernel,
        grid=(bsz, B_HEADS // 2, seq // TQA),
        in_specs=[pl.BlockSpec((None, TQA, 2 * LANES), lambda b, g, i: (b, i, g)),
                  pl.BlockSpec((None, seq, 2 * LANES), lambda b, g, i: (b, 0, g)),
                  pl.BlockSpec((None, seq, LANES), lambda b, g, i: (b, 0, 0)),
                  pl.BlockSpec((None, nblk, 2 * LANES, TK), lambda b, g, i: (b, 0, g, 0))],
        out_specs=pl.BlockSpec((None, TQA, LANES), lambda b, g, i: (b, i, g)),
        out_shape=jax.ShapeDtypeStruct((bsz, seq, B_HEADS * B_V), BF16),
        scratch_shapes=[pltpu.VMEM((2, LANES, TQA), F32)],
        compiler_params=_params("parallel", "parallel", "arbitrary"),
        name="mla",
    )(r3(qb), r3(kvb), r3(kpe), r4t(kvbt))

    wba = w_branch_a[0].astype(BF16)
    wbb = w_branch_b[0].astype(BF16)
    wout = w_out[0].astype(BF16)
    wpq = w_peer_query[0].astype(BF16)
    sk1 = peer_sub_keys_1[0].astype(BF16)
    sk2 = peer_sub_keys_2[0].astype(BF16)
    score_spec = pl.BlockSpec((PEER_HEADS, PEER_NKEYS, tm), lambda i: (0, 0, i))
    score_shape = jax.ShapeDtypeStruct((PEER_HEADS, PEER_NKEYS, t), F32)
    x1, h2, s1t, s2t = pl.pallas_call(
        _merge_kernel,
        grid=(t // tm,),
        in_specs=[tok(d), tok(A_WIDTH), tok(B_HEADS * B_V), tok(2 * d), _full(wba.shape),
                  _full(wbb.shape), _full(wout.shape), _full((1, d)), _full(wpq.shape),
                  _full(sk1.shape), _full(sk2.shape)],
        out_specs=[tok(d), tok(d), score_spec, score_spec],
        out_shape=[jax.ShapeDtypeStruct((t, d), F32), bf(d), score_shape, score_shape],
        compiler_params=_params("parallel"),
        name="merge",
    )(x2d, ya.reshape(t, A_WIDTH), yb.reshape(t, B_HEADS * B_V), sg, wba, wbb, wout,
      row(ffn_norm_g[0]), wpq, sk1, sk2)

    tl = TL_ROUTE
    hs = pl.BlockSpec((PEER_HEADS, PEER_NKEYS, tl), lambda i: (0, 0, i))
    a_co, e2, tau = pl.pallas_call(
        _route_kernel,
        grid=(t // tl,),
        in_specs=[hs, hs],
        out_specs=[hs, hs, pl.BlockSpec((PEER_HEADS, 1, tl), lambda i: (0, 0, i))],
        out_shape=[score_shape, score_shape, jax.ShapeDtypeStruct((PEER_HEADS, 1, t), F32)],
        scratch_shapes=[pltpu.VMEM((PEER_TOPK, tl), F32)] * 3,
        compiler_params=_params("parallel"),
        name="route",
    )(s1t, s2t)

    u_bf = peer_expert_u[0].astype(BF16)
    vt_bf = peer_expert_v[0].T.astype(BF16)
    tt, te = TT_EXP, TE_EXP
    rt = pl.BlockSpec((PEER_HEADS, PEER_NKEYS, tt), lambda i, e: (0, 0, i))
    n_tiles = n_exp // te
    out = pl.pallas_call(
        _expert_kernel,
        grid=(t // tt, n_tiles + 1),
        in_specs=[pl.BlockSpec((tt, d), lambda i, e: (i, 0)),
                  pl.BlockSpec((te, d), lambda i, e: (jnp.minimum(e, n_tiles - 1), 0)),
                  pl.BlockSpec((d, te), lambda i, e: (0, jnp.maximum(e - 1, 0))),
                  rt, rt, rt, rt,
                  pl.BlockSpec((PEER_HEADS, 1, tt), lambda i, e: (0, 0, i)),
                  pl.BlockSpec((tt, d), lambda i, e: (i, 0)),
                  pl.BlockSpec((1, d), lambda i, e: (0, 0))],
        out_specs=pl.BlockSpec((tt, d), lambda i, e: (i, 0)),
        out_shape=jax.ShapeDtypeStruct((t, d), F32),
        scratch_shapes=[pltpu.VMEM((d, tt), F32), pltpu.VMEM((te, tt), F32),
                        pltpu.VMEM((te, tt), F32), pltpu.VMEM((te, tt), BF16)],
        compiler_params=_params("parallel", "arbitrary"),
        name="experts",
    )(h2, u_bf, vt_bf, s1t, s2t, a_co, e2, tau, x1, row(final_norm_g))
    return out.reshape(bsz, seq, d)
```

```python
import math

import jax
import jax.numpy as jnp
from jax import lax
from jax.experimental import pallas as pl
from jax.experimental.pallas import tpu as pltpu

F32 = jnp.float32
BF16 = jnp.bfloat16

D_MODEL = 1024
A_HEADS = 8
A_HEAD_DIM = 64
A_WIDTH = A_HEADS * A_HEAD_DIM
MOBA_BLOCK = 256
MOBA_TOPK = 3
B_HEADS = 8
B_NOPE = 64
B_ROPE = 32
B_V = 64
Q_LORA = 256
KV_LORA = 128
PEER_HEADS = 8
PEER_NKEYS = 128
PEER_HALF = 128
PEER_TOPK = 16
ROPE_THETA = 10000.0
RMS_EPS = 1e-6

LANES = 128
NEG_BIG = -1e30
LOG2E = math.log2(math.e)
VMEM_LIMIT = 56 * 1024 * 1024

TM_PROJ = 256
TK = MOBA_BLOCK
TQA = 4 * TK
TL_ROUTE = 256
TT_EXP = 512
TE_EXP = 512
TB_EXP = 256
TJ_EXP = 16


def _dot(a, b):
    return jnp.dot(a, b, preferred_element_type=F32)


def _dot_nt(a, b):
    return lax.dot_general(a, b, (((1,), (1,)), ((), ())), preferred_element_type=F32)


def _rms(x, g):
    return x * lax.rsqrt(jnp.mean(x * x, axis=-1, keepdims=True) + RMS_EPS) * g


def _inproj_kernel(x_ref, g_ref, wqkv_ref, wc_ref, wg_ref, wqup_ref, wkvup_ref, qng_ref,
                   kvng_ref, ca_ref, sa1_ref, sa2_ref, cb_ref, sb1_ref, sb2_ref,
                   qa_ref, ka_ref, vat_ref, qb_ref, kvb_ref, kvbt_ref, kpe_ref, sg_ref):
    h = _rms(x_ref[...], g_ref[...]).astype(BF16)

    qkv = _dot(h, wqkv_ref[...])
    ca, sa1, sa2 = ca_ref[...], sa1_ref[...], sa2_ref[...]

    def rope_a(xg):
        return xg * ca + pltpu.roll(xg, LANES - 32, 1) * sa1 + pltpu.roll(xg, 32, 1) * sa2

    for grp in range(A_WIDTH // LANES):
        lo = grp * LANES
        qa_ref[:, lo:lo + LANES] = (rope_a(qkv[:, lo:lo + LANES]) * (0.125 * LOG2E)).astype(BF16)
        ka_ref[:, lo:lo + LANES] = rope_a(qkv[:, A_WIDTH + lo:A_WIDTH + lo + LANES]).astype(BF16)
    vat_ref[...] = qkv[:, 2 * A_WIDTH:].T.astype(BF16)

    c = _dot(h, wc_ref[...])
    cb, sb1, sb2 = cb_ref[...], sb1_ref[...], sb2_ref[...]

    def rope_b(xg):
        return xg * cb + pltpu.roll(xg, LANES - 16, 1) * sb1 + pltpu.roll(xg, 16, 1) * sb2

    cq = _rms(c[:, :Q_LORA], qng_ref[...]).astype(BF16)
    qb = _dot(cq, wqup_ref[...]) * (LOG2E / math.sqrt(B_NOPE + B_ROPE))
    for hd in range(B_HEADS):
        lo = hd * LANES
        qb_ref[:, lo:lo + LANES] = rope_b(qb[:, lo:lo + LANES]).astype(BF16)
    ckv = _rms(c[:, Q_LORA:Q_LORA + KV_LORA], kvng_ref[...]).astype(BF16)
    kv = _dot(ckv, wkvup_ref[...])
    kvb_ref[...] = kv.astype(BF16)
    kvbt_ref[...] = kv.T.astype(BF16)
    kpe_ref[...] = rope_b(c[:, Q_LORA + KV_LORA:]).astype(BF16)

    sg_ref[...] = jax.nn.sigmoid(_dot(h, wg_ref[...])).astype(BF16)


def _attn_step(scores, carry, acc_ref, vts):
    new, ps, alphas = [], [], []
    for s, (m, l) in zip(scores, carry):
        m_new = jnp.maximum(m, jnp.max(s, axis=0, keepdims=True))
        alpha = jnp.exp2(m - m_new)
        p = jnp.exp2(s - m_new)
        new.append((m_new, alpha * l + jnp.sum(p, axis=0, keepdims=True)))
        ps.append(p.astype(BF16))
        alphas.append(alpha)
    for hh in range(2):
        acc_ref[hh] = alphas[hh] * acc_ref[hh] + _dot(vts[hh], ps[hh])
    return tuple(new)


def _attn_loop(i, step):
    per = TQA // TK
    init = tuple((jnp.full((1, TQA), NEG_BIG, F32), jnp.zeros((1, TQA), F32)) for _ in range(2))
    carry = lax.fori_loop(0, per * i, lambda j, c: step(j, c, None), init)
    krow = lax.broadcasted_iota(jnp.int32, (TK, TQA), 0)
    qcol = lax.broadcasted_iota(jnp.int32, (TK, TQA), 1)
    for a in range(per):
        j = per * i + a
        carry = step(j, carry, (j * TK + krow) <= (i * TQA + qcol))
    return carry


def _transpose_bf16(x):
    return x.astype(F32).T.astype(BF16)


def _moba_kernel(q_ref, k_ref, vt_ref, o_ref, kmh_ref, kml_ref, acc_ref):
    i = pl.program_id(2)
    nblk = k_ref.shape[0] // TK

    @pl.when(i == 0)
    def _():
        rows = lax.broadcasted_iota(jnp.int32, (LANES, LANES), 0)
        km = jnp.zeros((LANES, LANES), F32)
        for j in range(nblk):
            blk = k_ref[j * TK:(j + 1) * TK, :].astype(F32)
            km = jnp.where(rows == j, jnp.mean(blk, axis=0, keepdims=True), km)
        hi = km.astype(BF16)
        kmh_ref[...] = hi
        kml_ref[...] = (km - hi.astype(F32)).astype(BF16)

    qt = _transpose_bf16(q_ref[...])
    sub = lax.broadcasted_iota(jnp.int32, (LANES, TQA), 0)
    col = lax.broadcasted_iota(jnp.int32, (LANES, TQA), 1)
    own = (TQA // TK) * i + col // TK
    zero = jnp.zeros_like(qt)
    qt_heads = (jnp.where(sub < A_HEAD_DIM, qt, zero), jnp.where(sub >= A_HEAD_DIM, qt, zero))

    q_aug = []
    for qh in qt_heads:
        gate = _dot(kmh_ref[...], qh) + _dot(kml_ref[...], qh)
        valid = sub < own
        g = jnp.where(valid, gate, -jnp.inf)[:nblk]
        thr = jnp.max(g, axis=0, keepdims=True)
        for _ in range(MOBA_TOPK - 1):
            g = jnp.where(g >= thr, -jnp.inf, g)
            thr = jnp.max(g, axis=0, keepdims=True)
        keep = (valid & (gate >= thr)) | (sub == own)
        bias = jnp.where(keep, 0.0, NEG_BIG).astype(BF16)
        q_aug.append(jnp.concatenate([qh, bias], axis=0))

    lane = lax.broadcasted_iota(jnp.int32, (TK, LANES), 1)
    acc_ref[...] = jnp.zeros_like(acc_ref)

    def step(j, carry, causal):
        start = pl.multiple_of(j * TK, TK)
        onehot = jnp.where(lane == j, 1.0, 0.0).astype(BF16)
        ka = jnp.concatenate([k_ref[pl.ds(start, TK), :], onehot], axis=1)
        scores = [_dot(ka, qa) for qa in q_aug]
        if causal is not None:
            scores = [jnp.where(causal, s, NEG_BIG) for s in scores]
        vt = vt_ref[j]
        return _attn_step(scores, carry, acc_ref, (vt, vt))

    carry = _attn_loop(i, step)
    outs = [acc_ref[hh] / carry[hh][1] for hh in range(2)]
    o_ref[...] = jnp.where(sub < A_HEAD_DIM, outs[0], outs[1]).T.astype(o_ref.dtype)


def _mla_kernel(q_ref, kv_ref, kpe_ref, kvt_ref, o_ref, acc_ref):
    i = pl.program_id(2)
    lane = lax.broadcasted_iota(jnp.int32, (TK, LANES), 1)
    qts = (_transpose_bf16(q_ref[:, :LANES]), _transpose_bf16(q_ref[:, LANES:]))
    acc_ref[...] = jnp.zeros_like(acc_ref)

    def step(j, carry, causal):
        start = pl.multiple_of(j * TK, TK)
        kpe = kpe_ref[pl.ds(start, TK), :]
        scores = []
        for hh in range(2):
            kv = kv_ref[pl.ds(start, TK), hh * LANES:(hh + 1) * LANES]
            key = jnp.where(lane < B_NOPE, kv, kpe)
            scores.append(_dot(key, qts[hh]))
        if causal is not None:
            scores = [jnp.where(causal, s, NEG_BIG) for s in scores]
        return _attn_step(scores, carry, acc_ref, (kvt_ref[j, :LANES, :], kvt_ref[j, LANES:, :]))

    carry = _attn_loop(i, step)
    outs = [(acc_ref[hh] / carry[hh][1])[B_NOPE:, :] for hh in range(2)]
    o_ref[...] = jnp.concatenate(outs, axis=0).T.astype(o_ref.dtype)


def _merge_kernel(x_ref, ya_ref, yb_ref, sg_ref, wba_ref, wbb_ref, wout_ref, fg_ref, wpq_ref,
                  sk1_ref, sk2_ref, x1_ref, h2_ref, s1_ref, s2_ref):
    merged = (sg_ref[:, :D_MODEL].astype(F32) * _dot(ya_ref[...], wba_ref[...])
              + sg_ref[:, D_MODEL:].astype(F32) * _dot(yb_ref[...], wbb_ref[...]))
    x1 = x_ref[...] + _dot(merged.astype(BF16), wout_ref[...])
    x1_ref[...] = x1
    h2 = _rms(x1, fg_ref[...]).astype(BF16)
    h2_ref[...] = h2
    qp = _dot(h2, wpq_ref[...]).astype(BF16)
    for hd in range(PEER_HEADS):
        lo = 2 * hd * PEER_HALF
        s1_ref[hd] = _dot_nt(sk1_ref[hd], qp[:, lo:lo + PEER_HALF])
        s2_ref[hd] = _dot_nt(sk2_ref[hd], qp[:, lo + PEER_HALF:lo + 2 * PEER_HALF])


def _route_kernel(s1_ref, s2_ref, a_ref, e2_ref, tau_ref, v1_ref, v2_ref, best_ref):
    def top16(src, dst_ref):
        cur = src
        for r in range(PEER_TOPK):
            mx = jnp.max(cur, axis=0, keepdims=True)
            dst_ref[r:r + 1, :] = mx
            cur = jnp.where(cur == mx, -jnp.inf, cur)

    def head(hd, carry):
        s1 = s1_ref[hd]
        s2 = s2_ref[hd]
        top16(s1, v1_ref)
        top16(s2, v2_ref)
        v1 = v1_ref[...]
        v2 = v2_ref[...]
        half = PEER_TOPK // 2
        cand = jnp.concatenate(
            [v1 + v2[0:1]] + [v1[:half] + v2[r:r + 1] for r in range(1, half)]
            + [v1[0:1] + v2[half:]], axis=0)
        top16(cand, best_ref)
        best = best_ref[...]
        z = jnp.sum(jnp.exp(best - best[0:1, :]), axis=0, keepdims=True)
        tau_ref[hd] = best[PEER_TOPK - 1:PEER_TOPK, :]
        a_ref[hd] = jnp.exp(s1 - v1_ref[0:1, :]) / z
        e2_ref[hd] = jnp.exp(s2 - v2_ref[0:1, :])
        return carry

    lax.fori_loop(0, PEER_HEADS, head, 0)


def _expert_kernel(h2_ref, u_ref, vt_ref, s1_ref, s2_ref, a_ref, e2_ref, tau_ref, x1_ref, fg_ref,
                   o_ref, acc_ref, act0_ref, act1_ref, wa_ref):
    e = pl.program_id(1)
    n_tiles = pl.num_programs(1) - 1
    slabs = TE_EXP // PEER_NKEYS
    prev = jnp.maximum(e - 1, 0)

    @pl.when(e == 0)
    def _():
        acc_ref[...] = jnp.zeros_like(acc_ref)
        act1_ref[...] = jnp.zeros_like(act1_ref)

    def body(act_w_ref, act_r_ref):
        for tb in range(TT_EXP // TB_EXP):
            ts = slice(tb * TB_EXP, (tb + 1) * TB_EXP)
            pre = _dot_nt(u_ref[...], h2_ref[ts, :])
            act_w_ref[:, ts] = 0.5 * pre * (1.0 + lax.erf(pre * math.sqrt(0.5)))
            tau = [tau_ref[hd, :, ts] for hd in range(PEER_HEADS)]
            for jt in range(PEER_NKEYS // TJ_EXP):
                rj = slice(jt * TJ_EXP, (jt + 1) * TJ_EXP)
                w = [None] * slabs
                for hd in range(PEER_HEADS):
                    s2t = s2_ref[hd, rj, ts]
                    e2t = e2_ref[hd, rj, ts]
                    for sl in range(slabs):
                        i1 = prev * slabs + sl
                        cand = s2t + s1_ref[hd, pl.ds(i1, 1), ts]
                        term = jnp.where(cand >= tau[hd], e2t, 0.0) * a_ref[hd, pl.ds(i1, 1), ts]
                        w[sl] = term if w[sl] is None else w[sl] + term
                for sl in range(slabs):
                    ro = slice(sl * PEER_NKEYS + jt * TJ_EXP, sl * PEER_NKEYS + (jt + 1) * TJ_EXP)
                    wa_ref[ro, ts] = (w[sl] * act_r_ref[ro, ts]).astype(BF16)
            acc_ref[:, ts] += _dot(vt_ref[...], wa_ref[:, ts])

    @pl.when(e % 2 == 0)
    def _():
        body(act0_ref, act1_ref)

    @pl.when(e % 2 == 1)
    def _():
        body(act1_ref, act0_ref)

    @pl.when(e == n_tiles)
    def _():
        o_ref[...] = _rms(x1_ref[...] + acc_ref[...].T, fg_ref[...])


def _params(*sem):
    return pltpu.CompilerParams(dimension_semantics=sem, vmem_limit_bytes=VMEM_LIMIT)


def _full(shape):
    nd = len(shape)
    return pl.BlockSpec(shape, lambda *_: (0,) * nd)


def _rope_tables(positions):
    pos = positions.reshape(-1).astype(F32)[:, None]

    def cs(dim):
        inv_freq = ROPE_THETA ** (-jnp.arange(0, dim, 2, dtype=F32) / dim)
        ang = pos * inv_freq
        return jnp.cos(ang), jnp.sin(ang)

    ca, sa = cs(A_HEAD_DIM)
    za = jnp.zeros_like(sa)
    cb, sb = cs(B_ROPE)
    t = pos.shape[0]
    zb = lambda n: jnp.zeros((t, n), F32)
    ob = lambda n: jnp.ones((t, n), F32)
    return (jnp.concatenate([ca] * 4, axis=1),
            jnp.concatenate([-sa, za, -sa, za], axis=1),
            jnp.concatenate([za, sa, za, sa], axis=1),
            jnp.concatenate([ob(B_NOPE), cb, cb, ob(32)], axis=1),
            jnp.concatenate([zb(B_NOPE), -sb, zb(16), zb(32)], axis=1),
            jnp.concatenate([zb(B_NOPE), zb(16), sb, zb(32)], axis=1))


def kernel(x, positions, mix_norm_g, w_in, q_norm_g, w_q_up, kv_norm_g, w_kv_up, w_branch_a,
           w_branch_b, w_out, ffn_norm_g, w_peer_query, peer_sub_keys_1, peer_sub_keys_2,
           peer_expert_u, peer_expert_v, final_norm_g):
    bsz, seq, d = x.shape
    t = bsz * seq
    n_exp = peer_expert_u.shape[1]
    assert d == D_MODEL and seq % TQA == 0 and t % TT_EXP == 0 and n_exp % TE_EXP == 0
    assert mix_norm_g.shape[0] == 1, "single-layer block"

    w = w_in[0]
    o_cq = 3 * A_WIDTH
    o_ckv = o_cq + Q_LORA
    o_kpe = o_ckv + KV_LORA
    o_g = o_kpe + B_ROPE
    wqkv = w[:, :o_cq].astype(BF16)
    wc = jnp.concatenate([w[:, o_cq:o_kpe], jnp.zeros((d, B_NOPE), F32), w[:, o_kpe:o_g],
                          jnp.zeros((d, LANES - B_NOPE - B_ROPE), F32)], axis=1).astype(BF16)
    wg = w[:, o_g:].astype(BF16)
    wqup = jnp.pad(w_q_up[0].reshape(Q_LORA, B_HEADS, B_NOPE + B_ROPE),
                   ((0, 0), (0, 0), (0, LANES - B_NOPE - B_ROPE))).reshape(Q_LORA, B_HEADS * LANES)
    wqup = wqup.astype(BF16)
    wkvup = w_kv_up[0].astype(BF16)
    row = lambda v: v.reshape(1, -1).astype(F32)
    tables = _rope_tables(positions)
    x2d = x.reshape(t, d)

    tm = TM_PROJ
    assert tm == TK, "one projection tile is one attention key block (transposed value tiles)"
    tok = lambda width: pl.BlockSpec((tm, width), lambda i: (i, 0))
    bf = lambda width: jax.ShapeDtypeStruct((t, width), BF16)
    tok_t = lambda width: pl.BlockSpec((None, width, tm), lambda i: (i, 0, 0))
    bf_t = lambda width: jax.ShapeDtypeStruct((t // tm, width, tm), BF16)
    qa, ka, vat, qb, kvb, kvbt, kpe, sg = pl.pallas_call(
        _inproj_kernel,
        grid=(t // tm,),
        in_specs=[tok(d), _full((1, d)), _full(wqkv.shape), _full(wc.shape), _full(wg.shape),
                  _full(wqup.shape), _full(wkvup.shape), _full((1, Q_LORA)), _full((1, KV_LORA))]
                 + [tok(LANES)] * 6,
        out_specs=[tok(A_WIDTH), tok(A_WIDTH), tok_t(A_WIDTH), tok(B_HEADS * LANES),
                   tok(B_HEADS * LANES), tok_t(B_HEADS * LANES), tok(LANES), tok(2 * d)],
        out_shape=[bf(A_WIDTH), bf(A_WIDTH), bf_t(A_WIDTH), bf(B_HEADS * LANES),
                   bf(B_HEADS * LANES), bf_t(B_HEADS * LANES), bf(LANES), bf(2 * d)],
        compiler_params=_params("parallel"),
        name="inproj",
    )(x2d, row(mix_norm_g[0]), wqkv, wc, wg, wqup, wkvup, row(q_norm_g[0]), row(kv_norm_g[0]),
      *tables)

    r3 = lambda a: a.reshape(bsz, seq, a.shape[-1])
    r4t = lambda a: a.reshape(bsz, seq // tm, a.shape[1], tm)
    nblk = seq // TK
    assert nblk <= LANES
    pairs = A_WIDTH // LANES
    ya = pl.pallas_call(
        _moba_kernel,
        grid=(bsz, pairs, seq // TQA),
        in_specs=[pl.BlockSpec((None, TQA, LANES), lambda b, g, i: (b, i, g)),
                  pl.BlockSpec((None, seq, LANES), lambda b, g, i: (b, 0, g)),
                  pl.BlockSpec((None, nblk, LANES, TK), lambda b, g, i: (b, 0, g, 0))],
        out_specs=pl.BlockSpec((None, TQA, LANES), lambda b, g, i: (b, i, g)),
        out_shape=jax.ShapeDtypeStruct((bsz, seq, A_WIDTH), BF16),
        scratch_shapes=[pltpu.VMEM((LANES, LANES), BF16), pltpu.VMEM((LANES, LANES), BF16),
                        pltpu.VMEM((2, LANES, TQA), F32)],
        compiler_params=_params("parallel", "parallel", "arbitrary"),
        name="moba",
    )(r3(qa), r3(ka), r4t(vat))

    yb = pl.pallas_call(
        _mla_kernel,
        grid=(bsz, B_HEADS // 2, seq // TQA),
        in_specs=[pl.BlockSpec((None, TQA, 2 * LANES), lambda b, g, i: (b, i, g)),
                  pl.BlockSpec((None, seq, 2 * LANES), lambda b, g, i: (b, 0, g)),
                  pl.BlockSpec((None, seq, LANES), lambda b, g, i: (b, 0, 0)),
                  pl.BlockSpec((None, nblk, 2 * LANES, TK), lambda b, g, i: (b, 0, g, 0))],
        out_specs=pl.BlockSpec((None, TQA, LANES), lambda b, g, i: (b, i, g)),
        out_shape=jax.ShapeDtypeStruct((bsz, seq, B_HEADS * B_V), BF16),
        scratch_shapes=[pltpu.VMEM((2, LANES, TQA), F32)],
        compiler_params=_params("parallel", "parallel", "arbitrary"),
        name="mla",
    )(r3(qb), r3(kvb), r3(kpe), r4t(kvbt))

    wba = w_branch_a[0].astype(BF16)
    wbb = w_branch_b[0].astype(BF16)
    wout = w_out[0].astype(BF16)
    wpq = w_peer_query[0].astype(BF16)
    sk1 = peer_sub_keys_1[0].astype(BF16)
    sk2 = peer_sub_keys_2[0].astype(BF16)
    score_spec = pl.BlockSpec((PEER_HEADS, PEER_NKEYS, tm), lambda i: (0, 0, i))
    score_shape = jax.ShapeDtypeStruct((PEER_HEADS, PEER_NKEYS, t), F32)
    x1, h2, s1t, s2t = pl.pallas_call(
        _merge_kernel,
        grid=(t // tm,),
        in_specs=[tok(d), tok(A_WIDTH), tok(B_HEADS * B_V), tok(2 * d), _full(wba.shape),
                  _full(wbb.shape), _full(wout.shape), _full((1, d)), _full(wpq.shape),
                  _full(sk1.shape), _full(sk2.shape)],
        out_specs=[tok(d), tok(d), score_spec, score_spec],
        out_shape=[jax.ShapeDtypeStruct((t, d), F32), bf(d), score_shape, score_shape],
        compiler_params=_params("parallel"),
        name="merge",
    )(x2d, ya.reshape(t, A_WIDTH), yb.reshape(t, B_HEADS * B_V), sg, wba, wbb, wout,
      row(ffn_norm_g[0]), wpq, sk1, sk2)

    tl = TL_ROUTE
    hs = pl.BlockSpec((PEER_HEADS, PEER_NKEYS, tl), lambda i: (0, 0, i))
    a_co, e2, tau = pl.pallas_call(
        _route_kernel,
        grid=(t // tl,),
        in_specs=[hs, hs],
        out_specs=[hs, hs, pl.BlockSpec((PEER_HEADS, 1, tl), lambda i: (0, 0, i))],
        out_shape=[score_shape, score_shape, jax.ShapeDtypeStruct((PEER_HEADS, 1, t), F32)],
        scratch_shapes=[pltpu.VMEM((PEER_TOPK, tl), F32)] * 3,
        compiler_params=_params("parallel"),
        name="route",
    )(s1t, s2t)

    u_bf = peer_expert_u[0].astype(BF16)
    vt_bf = peer_expert_v[0].T.astype(BF16)
    tt, te = TT_EXP, TE_EXP
    rt = pl.BlockSpec((PEER_HEADS, PEER_NKEYS, tt), lambda i, e: (0, 0, i))
    n_tiles = n_exp // te
    out = pl.pallas_call(
        _expert_kernel,
        grid=(t // tt, n_tiles + 1),
        in_specs=[pl.BlockSpec((tt, d), lambda i, e: (i, 0)),
                  pl.BlockSpec((te, d), lambda i, e: (jnp.minimum(e, n_tiles - 1), 0)),
                  pl.BlockSpec((d, te), lambda i, e: (0, jnp.maximum(e - 1, 0))),
                  rt, rt, rt, rt,
                  pl.BlockSpec((PEER_HEADS, 1, tt), lambda i, e: (0, 0, i)),
                  pl.BlockSpec((tt, d), lambda i, e: (i, 0)),
                  pl.BlockSpec((1, d), lambda i, e: (0, 0))],
        out_specs=pl.BlockSpec((tt, d), lambda i, e: (i, 0)),
        out_shape=jax.ShapeDtypeStruct((t, d), F32),
        scratch_shapes=[pltpu.VMEM((d, tt), F32), pltpu.VMEM((te, tt), F32),
                        pltpu.VMEM((te, tt), F32), pltpu.VMEM((te, tt), BF16)],
        compiler_params=_params("parallel", "arbitrary"),
        name="experts",
    )(h2, u_bf, vt_bf, s1t, s2t, a_co, e2, tau, x1, row(final_norm_g))
    return out.reshape(bsz, seq, d)
```

```python
import math

import jax
import jax.numpy as jnp
from jax import lax
from jax.experimental import pallas as pl
from jax.experimental.pallas import tpu as pltpu

F32 = jnp.float32
BF16 = jnp.bfloat16

D_MODEL = 1024
A_HEADS = 8
A_HEAD_DIM = 64
A_WIDTH = A_HEADS * A_HEAD_DIM
MOBA_BLOCK = 256
MOBA_TOPK = 3
B_HEADS = 8
B_NOPE = 64
B_ROPE = 32
B_V = 64
Q_LORA = 256
KV_LORA = 128
PEER_HEADS = 8
PEER_NKEYS = 128
PEER_HALF = 128
PEER_TOPK = 16
ROPE_THETA = 10000.0
RMS_EPS = 1e-6

LANES = 128
NEG_BIG = -1e30
LOG2E = math.log2(math.e)
VMEM_LIMIT = 56 * 1024 * 1024

TM_PROJ = 256
TK = MOBA_BLOCK
TQA = 4 * TK
TL_ROUTE = 256
TT_EXP = 512
TE_EXP = 512
TB_EXP = 256
TJ_EXP = 16


def _dot(a, b):
    return jnp.dot(a, b, preferred_element_type=F32)


def _dot_nt(a, b):
    return lax.dot_general(a, b, (((1,), (1,)), ((), ())), preferred_element_type=F32)


def _rms(x, g):
    return x * lax.rsqrt(jnp.mean(x * x, axis=-1, keepdims=True) + RMS_EPS) * g


def _inproj_kernel(x_ref, g_ref, wqkv_ref, wc_ref, wg_ref, wqup_ref, wkvup_ref, qng_ref,
                   kvng_ref, ca_ref, sa1_ref, sa2_ref, cb_ref, sb1_ref, sb2_ref,
                   qa_ref, ka_ref, vat_ref, qb_ref, kvb_ref, kvbt_ref, kpe_ref, sg_ref):
    h = _rms(x_ref[...], g_ref[...]).astype(BF16)

    qkv = _dot(h, wqkv_ref[...])
    ca, sa1, sa2 = ca_ref[...], sa1_ref[...], sa2_ref[...]

    def rope_a(xg):
        return xg * ca + pltpu.roll(xg, LANES - 32, 1) * sa1 + pltpu.roll(xg, 32, 1) * sa2

    for grp in range(A_WIDTH // LANES):
        lo = grp * LANES
        qa_ref[:, lo:lo + LANES] = (rope_a(qkv[:, lo:lo + LANES]) * (0.125 * LOG2E)).astype(BF16)
        ka_ref[:, lo:lo + LANES] = rope_a(qkv[:, A_WIDTH + lo:A_WIDTH + lo + LANES]).astype(BF16)
    vat_ref[...] = qkv[:, 2 * A_WIDTH:].T.astype(BF16)

    c = _dot(h, wc_ref[...])
    cb, sb1, sb2 = cb_ref[...], sb1_ref[...], sb2_ref[...]

    def rope_b(xg):
        return xg * cb + pltpu.roll(xg, LANES - 16, 1) * sb1 + pltpu.roll(xg, 16, 1) * sb2

    cq = _rms(c[:, :Q_LORA], qng_ref[...]).astype(BF16)
    qb = _dot(cq, wqup_ref[...]) * (LOG2E / math.sqrt(B_NOPE + B_ROPE))
    for hd in range(B_HEADS):
        lo = hd * LANES
        qb_ref[:, lo:lo + LANES] = rope_b(qb[:, lo:lo + LANES]).astype(BF16)
    ckv = _rms(c[:, Q_LORA:Q_LORA + KV_LORA], kvng_ref[...]).astype(BF16)
    kv = _dot(ckv, wkvup_ref[...])
    kvb_ref[...] = kv.astype(BF16)
    kvbt_ref[...] = kv.T.astype(BF16)
    kpe_ref[...] = rope_b(c[:, Q_LORA + KV_LORA:]).astype(BF16)

    sg_ref[...] = jax.nn.sigmoid(_dot(h, wg_ref[...])).astype(BF16)


def _attn_step(scores, carry, acc_ref, vts, c0=0):
    new, ps, alphas = [], [], []
    for s, (m, l) in zip(scores, carry):
        m_new = jnp.maximum(m[:, c0:], jnp.max(s, axis=0, keepdims=True))
        alpha = jnp.exp2(m[:, c0:] - m_new)
        p = jnp.exp2(s - m_new)
        l_new = alpha * l[:, c0:] + jnp.sum(p, axis=0, keepdims=True)
        if c0:
            m_new = jnp.concatenate([m[:, :c0], m_new], axis=1)
            l_new = jnp.concatenate([l[:, :c0], l_new], axis=1)
        new.append((m_new, l_new))
        ps.append(p.astype(BF16))
        alphas.append(alpha)
    for hh in range(2):
        acc_ref[hh, :, c0:] = alphas[hh] * acc_ref[hh, :, c0:] + _dot(vts[hh], ps[hh])
    return tuple(new)


def _attn_loop(i, step):
    per = TQA // TK
    init = tuple((jnp.full((1, TQA), NEG_BIG, F32), jnp.zeros((1, TQA), F32)) for _ in range(2))
    carry = lax.fori_loop(0, per * i, lambda j, c: step(j, c, None, 0), init)
    for a in range(per):
        c0 = a * TK
        krow = lax.broadcasted_iota(jnp.int32, (TK, TQA - c0), 0)
        qcol = lax.broadcasted_iota(jnp.int32, (TK, TQA - c0), 1)
        carry = step(per * i + a, carry, krow <= qcol, c0)
    return carry


def _transpose_bf16(x):
    return x.astype(F32).T.astype(BF16)


def _moba_kernel(q_ref, k_ref, vt_ref, o_ref, kmh_ref, kml_ref, acc_ref):
    i = pl.program_id(2)
    nblk = k_ref.shape[0] // TK

    @pl.when(i == 0)
    def _():
        rows = lax.broadcasted_iota(jnp.int32, (LANES, LANES), 0)
        km = jnp.zeros((LANES, LANES), F32)
        for j in range(nblk):
            blk = k_ref[j * TK:(j + 1) * TK, :].astype(F32)
            km = jnp.where(rows == j, jnp.mean(blk, axis=0, keepdims=True), km)
        hi = km.astype(BF16)
        kmh_ref[...] = hi
        kml_ref[...] = (km - hi.astype(F32)).astype(BF16)

    qt = _transpose_bf16(q_ref[...])
    sub = lax.broadcasted_iota(jnp.int32, (LANES, TQA), 0)
    col = lax.broadcasted_iota(jnp.int32, (LANES, TQA), 1)
    own = (TQA // TK) * i + col // TK
    zero = jnp.zeros_like(qt)
    qt_heads = (jnp.where(sub < A_HEAD_DIM, qt, zero), jnp.where(sub >= A_HEAD_DIM, qt, zero))

    q_aug = []
    for qh in qt_heads:
        gate = _dot(kmh_ref[...], qh) + _dot(kml_ref[...], qh)
        valid = sub < own
        g = jnp.where(valid, gate, -jnp.inf)[:nblk]
        thr = jnp.max(g, axis=0, keepdims=True)
        for _ in range(MOBA_TOPK - 1):
            g = jnp.where(g >= thr, -jnp.inf, g)
            thr = jnp.max(g, axis=0, keepdims=True)
        keep = (valid & (gate >= thr)) | (sub == own)
        bias = jnp.where(keep, 0.0, NEG_BIG).astype(BF16)
        q_aug.append(jnp.concatenate([qh, bias], axis=0))

    lane = lax.broadcasted_iota(jnp.int32, (TK, LANES), 1)
    acc_ref[...] = jnp.zeros_like(acc_ref)

    def step(j, carry, causal, c0):
        start = pl.multiple_of(j * TK, TK)
        onehot = jnp.where(lane == j, 1.0, 0.0).astype(BF16)
        ka = jnp.concatenate([k_ref[pl.ds(start, TK), :], onehot], axis=1)
        scores = [_dot(ka, qa[:, c0:]) for qa in q_aug]
        if causal is not None:
            scores = [jnp.where(causal, s, NEG_BIG) for s in scores]
        vt = vt_ref[j]
        return _attn_step(scores, carry, acc_ref, (vt, vt), c0)

    carry = _attn_loop(i, step)
    outs = [acc_ref[hh] / carry[hh][1] for hh in range(2)]
    o_ref[...] = jnp.where(sub < A_HEAD_DIM, outs[0], outs[1]).T.astype(o_ref.dtype)


def _mla_kernel(q_ref, kv_ref, kpe_ref, kvt_ref, o_ref, acc_ref):
    i = pl.program_id(2)
    lane = lax.broadcasted_iota(jnp.int32, (TK, LANES), 1)
    qts = (_transpose_bf16(q_ref[:, :LANES]), _transpose_bf16(q_ref[:, LANES:]))
    acc_ref[...] = jnp.zeros_like(acc_ref)

    def step(j, carry, causal, c0):
        start = pl.multiple_of(j * TK, TK)
        kpe = kpe_ref[pl.ds(start, TK), :]
        scores = []
        for hh in range(2):
            kv = kv_ref[pl.ds(start, TK), hh * LANES:(hh + 1) * LANES]
            key = jnp.where(lane < B_NOPE, kv, kpe)
            scores.append(_dot(key, qts[hh][:, c0:]))
        if causal is not None:
            scores = [jnp.where(causal, s, NEG_BIG) for s in scores]
        return _attn_step(scores, carry, acc_ref,
                          (kvt_ref[j, :LANES, :], kvt_ref[j, LANES:, :]), c0)

    carry = _attn_loop(i, step)
    outs = [(acc_ref[hh] / carry[hh][1])[B_NOPE:, :] for hh in range(2)]
    o_ref[...] = jnp.concatenate(outs, axis=0).T.astype(o_ref.dtype)


def _merge_kernel(x_ref, ya_ref, yb_ref, sg_ref, wba_ref, wbb_ref, wout_ref, fg_ref, wpq_ref,
                  sk1_ref, sk2_ref, x1_ref, h2_ref, s1_ref, s2_ref):
    merged = (sg_ref[:, :D_MODEL].astype(F32) * _dot(ya_ref[...], wba_ref[...])
              + sg_ref[:, D_MODEL:].astype(F32) * _dot(yb_ref[...], wbb_ref[...]))
    x1 = x_ref[...] + _dot(merged.astype(BF16), wout_ref[...])
    x1_ref[...] = x1
    h2 = _rms(x1, fg_ref[...]).astype(BF16)
    h2_ref[...] = h2
    qp = _dot(h2, wpq_ref[...]).astype(BF16)
    for hd in range(PEER_HEADS):
        lo = 2 * hd * PEER_HALF
        s1_ref[hd] = _dot_nt(sk1_ref[hd], qp[:, lo:lo + PEER_HALF])
        s2_ref[hd] = _dot_nt(sk2_ref[hd], qp[:, lo + PEER_HALF:lo + 2 * PEER_HALF])


def _route_kernel(s1_ref, s2_ref, a_ref, e2_ref, tau_ref, v1_ref, v2_ref, best_ref):
    def top16(src, dst_ref):
        cur = src
        for r in range(PEER_TOPK):
            mx = jnp.max(cur, axis=0, keepdims=True)
            dst_ref[r:r + 1, :] = mx
            cur = jnp.where(cur == mx, -jnp.inf, cur)

    def head(hd, carry):
        s1 = s1_ref[hd]
        s2 = s2_ref[hd]
        top16(s1, v1_ref)
        top16(s2, v2_ref)
        v1 = v1_ref[...]
        v2 = v2_ref[...]
        half = PEER_TOPK // 2
        cand = jnp.concatenate(
            [v1 + v2[0:1]] + [v1[:half] + v2[r:r + 1] for r in range(1, half)]
            + [v1[0:1] + v2[half:]], axis=0)
        top16(cand, best_ref)
        best = best_ref[...]
        z = jnp.sum(jnp.exp(best - best[0:1, :]), axis=0, keepdims=True)
        tau_ref[hd] = best[PEER_TOPK - 1:PEER_TOPK, :]
        a_ref[hd] = jnp.exp(s1 - v1_ref[0:1, :]) / z
        e2_ref[hd] = jnp.exp(s2 - v2_ref[0:1, :])
        return carry

    lax.fori_loop(0, PEER_HEADS, head, 0)


def _expert_kernel(h2_ref, u_ref, vt_ref, s1_ref, s2_ref, a_ref, e2_ref, tau_ref, x1_ref, fg_ref,
                   o_ref, acc_ref, act0_ref, act1_ref, wa_ref):
    e = pl.program_id(1)
    n_tiles = pl.num_programs(1) - 1
    slabs = TE_EXP // PEER_NKEYS
    prev = jnp.maximum(e - 1, 0)

    @pl.when(e == 0)
    def _():
        acc_ref[...] = jnp.zeros_like(acc_ref)
        act1_ref[...] = jnp.zeros_like(act1_ref)

    def body(act_w_ref, act_r_ref):
        for tb in range(TT_EXP // TB_EXP):
            ts = slice(tb * TB_EXP, (tb + 1) * TB_EXP)
            pre = _dot_nt(u_ref[...], h2_ref[ts, :])
            act_w_ref[:, ts] = 0.5 * pre * (1.0 + lax.erf(pre * math.sqrt(0.5)))
            tau = [tau_ref[hd, :, ts] for hd in range(PEER_HEADS)]
            for sl in range(slabs):
                i1 = prev * slabs + sl
                s1r = [s1_ref[hd, pl.ds(i1, 1), ts] for hd in range(PEER_HEADS)]
                ar = [a_ref[hd, pl.ds(i1, 1), ts] for hd in range(PEER_HEADS)]
                for jt in range(PEER_NKEYS // TJ_EXP):
                    rj = slice(jt * TJ_EXP, (jt + 1) * TJ_EXP)
                    w = None
                    for hd in range(PEER_HEADS):
                        cand = s2_ref[hd, rj, ts] + s1r[hd]
                        term = jnp.where(cand >= tau[hd], e2_ref[hd, rj, ts], 0.0) * ar[hd]
                        w = term if w is None else w + term
                    ro = slice(sl * PEER_NKEYS + jt * TJ_EXP, sl * PEER_NKEYS + (jt + 1) * TJ_EXP)
                    wa_ref[ro, ts] = (w * act_r_ref[ro, ts]).astype(BF16)
            acc_ref[:, ts] += _dot(vt_ref[...], wa_ref[:, ts])

    @pl.when(e % 2 == 0)
    def _():
        body(act0_ref, act1_ref)

    @pl.when(e % 2 == 1)
    def _():
        body(act1_ref, act0_ref)

    @pl.when(e == n_tiles)
    def _():
        o_ref[...] = _rms(x1_ref[...] + acc_ref[...].T, fg_ref[...])


def _params(*sem):
    return pltpu.CompilerParams(dimension_semantics=sem, vmem_limit_bytes=VMEM_LIMIT)


def _full(shape):
    nd = len(shape)
    return pl.BlockSpec(shape, lambda *_: (0,) * nd)


def _rope_tables(positions):
    pos = positions.reshape(-1).astype(F32)[:, None]

    def cs(dim):
        inv_freq = ROPE_THETA ** (-jnp.arange(0, dim, 2, dtype=F32) / dim)
        ang = pos * inv_freq
        return jnp.cos(ang), jnp.sin(ang)

    ca, sa = cs(A_HEAD_DIM)
    za = jnp.zeros_like(sa)
    cb, sb = cs(B_ROPE)
    t = pos.shape[0]
    zb = lambda n: jnp.zeros((t, n), F32)
    ob = lambda n: jnp.ones((t, n), F32)
    return (jnp.concatenate([ca] * 4, axis=1),
            jnp.concatenate([-sa, za, -sa, za], axis=1),
            jnp.concatenate([za, sa, za, sa], axis=1),
            jnp.concatenate([ob(B_NOPE), cb, cb, ob(32)], axis=1),
            jnp.concatenate([zb(B_NOPE), -sb, zb(16), zb(32)], axis=1),
            jnp.concatenate([zb(B_NOPE), zb(16), sb, zb(32)], axis=1))


def kernel(x, positions, mix_norm_g, w_in, q_norm_g, w_q_up, kv_norm_g, w_kv_up, w_branch_a,
           w_branch_b, w_out, ffn_norm_g, w_peer_query, peer_sub_keys_1, peer_sub_keys_2,
           peer_expert_u, peer_expert_v, final_norm_g):
    bsz, seq, d = x.shape
    t = bsz * seq
    n_exp = peer_expert_u.shape[1]
    assert d == D_MODEL and seq % TQA == 0 and t % TT_EXP == 0 and n_exp % TE_EXP == 0
    assert mix_norm_g.shape[0] == 1, "single-layer block"

    w = w_in[0]
    o_cq = 3 * A_WIDTH
    o_ckv = o_cq + Q_LORA
    o_kpe = o_ckv + KV_LORA
    o_g = o_kpe + B_ROPE
    wqkv = w[:, :o_cq].astype(BF16)
    wc = jnp.concatenate([w[:, o_cq:o_kpe], jnp.zeros((d, B_NOPE), F32), w[:, o_kpe:o_g],
                          jnp.zeros((d, LANES - B_NOPE - B_ROPE), F32)], axis=1).astype(BF16)
    wg = w[:, o_g:].astype(BF16)
    wqup = jnp.pad(w_q_up[0].reshape(Q_LORA, B_HEADS, B_NOPE + B_ROPE),
                   ((0, 0), (0, 0), (0, LANES - B_NOPE - B_ROPE))).reshape(Q_LORA, B_HEADS * LANES)
    wqup = wqup.astype(BF16)
    wkvup = w_kv_up[0].astype(BF16)
    row = lambda v: v.reshape(1, -1).astype(F32)
    tables = _rope_tables(positions)
    x2d = x.reshape(t, d)

    tm = TM_PROJ
    assert tm == TK, "one projection tile is one attention key block (transposed value tiles)"
    tok = lambda width: pl.BlockSpec((tm, width), lambda i: (i, 0))
    bf = lambda width: jax.ShapeDtypeStruct((t, width), BF16)
    tok_t = lambda width: pl.BlockSpec((None, width, tm), lambda i: (i, 0, 0))
    bf_t = lambda width: jax.ShapeDtypeStruct((t // tm, width, tm), BF16)
    qa, ka, vat, qb, kvb, kvbt, kpe, sg = pl.pallas_call(
        _inproj_kernel,
        grid=(t // tm,),
        in_specs=[tok(d), _full((1, d)), _full(wqkv.shape), _full(wc.shape), _full(wg.shape),
                  _full(wqup.shape), _full(wkvup.shape), _full((1, Q_LORA)), _full((1, KV_LORA))]
                 + [tok(LANES)] * 6,
        out_specs=[tok(A_WIDTH), tok(A_WIDTH), tok_t(A_WIDTH), tok(B_HEADS * LANES),
                   tok(B_HEADS * LANES), tok_t(B_HEADS * LANES), tok(LANES), tok(2 * d)],
        out_shape=[bf(A_WIDTH), bf(A_WIDTH), bf_t(A_WIDTH), bf(B_HEADS * LANES),
                   bf(B_HEADS * LANES), bf_t(B_HEADS * LANES), bf(LANES), bf(2 * d)],
        compiler_params=_params("parallel"),
        name="inproj",
    )(x2d, row(mix_norm_g[0]), wqkv, wc, wg, wqup, wkvup, row(q_norm_g[0]), row(kv_norm_g[0]),
      *tables)

    r3 = lambda a: a.reshape(bsz, seq, a.shape[-1])
    r4t = lambda a: a.reshape(bsz, seq // tm, a.shape[1], tm)
    nblk = seq // TK
    assert nblk <= LANES
    pairs = A_WIDTH // LANES
    ya = pl.pallas_call(
        _moba_kernel,
        grid=(bsz, pairs, seq // TQA),
        in_specs=[pl.BlockSpec((None, TQA, LANES), lambda b, g, i: (b, i, g)),
                  pl.BlockSpec((None, seq, LANES), lambda b, g, i: (b, 0, g)),
                  pl.BlockSpec((None, nblk, LANES, TK), lambda b, g, i: (b, 0, g, 0))],
        out_specs=pl.BlockSpec((None, TQA, LANES), lambda b, g, i: (b, i, g)),
        out_shape=jax.ShapeDtypeStruct((bsz, seq, A_WIDTH), BF16),
        scratch_shapes=[pltpu.VMEM((LANES, LANES), BF16), pltpu.VMEM((LANES, LANES), BF16),
                        pltpu.VMEM((2, LANES, TQA), F32)],
        compiler_params=_params("parallel", "parallel", "arbitrary"),
        name="moba",
    )(r3(qa), r3(ka), r4t(vat))

    yb = pl.pallas_call(
        _mla_kernel,
        grid=(bsz, B_HEADS // 2, seq // TQA),
        in_specs=[pl.BlockSpec((None, TQA, 2 * LANES), lambda b, g, i: (b, i, g)),
                  pl.BlockSpec((None, seq, 2 * LANES), lambda b, g, i: (b, 0, g)),
                  pl.BlockSpec((None, seq, LANES), lambda b, g, i: (b, 0, 0)),
                  pl.BlockSpec((None, nblk, 2 * LANES, TK), lambda b, g, i: (b, 0, g, 0))],
        out_specs=pl.BlockSpec((None, TQA, LANES), lambda b, g, i: (b, i, g)),
        out_shape=jax.ShapeDtypeStruct((bsz, seq, B_HEADS * B_V), BF16),
        scratch_shapes=[pltpu.VMEM((2, LANES, TQA), F32)],
        compiler_params=_params("parallel", "parallel", "arbitrary"),
        name="mla",
    )(r3(qb), r3(kvb), r3(kpe), r4t(kvbt))

    wba = w_branch_a[0].astype(BF16)
    wbb = w_branch_b[0].astype(BF16)
    wout = w_out[0].astype(BF16)
    wpq = w_peer_query[0].astype(BF16)
    sk1 = peer_sub_keys_1[0].astype(BF16)
    sk2 = peer_sub_keys_2[0].astype(BF16)
    score_spec = pl.BlockSpec((PEER_HEADS, PEER_NKEYS, tm), lambda i: (0, 0, i))
    score_shape = jax.ShapeDtypeStruct((PEER_HEADS, PEER_NKEYS, t), F32)
    x1, h2, s1t, s2t = pl.pallas_call(
        _merge_kernel,
        grid=(t // tm,),
        in_specs=[tok(d), tok(A_WIDTH), tok(B_HEADS * B_V), tok(2 * d), _full(wba.shape),
                  _full(wbb.shape), _full(wout.shape), _full((1, d)), _full(wpq.shape),
                  _full(sk1.shape), _full(sk2.shape)],
        out_specs=[tok(d), tok(d), score_spec, score_spec],
        out_shape=[jax.ShapeDtypeStruct((t, d), F32), bf(d), score_shape, score_shape],
        compiler_params=_params("parallel"),
        name="merge",
    )(x2d, ya.reshape(t, A_WIDTH), yb.reshape(t, B_HEADS * B_V), sg, wba, wbb, wout,
      row(ffn_norm_g[0]), wpq, sk1, sk2)

    tl = TL_ROUTE
    hs = pl.BlockSpec((PEER_HEADS, PEER_NKEYS, tl), lambda i: (0, 0, i))
    a_co, e2, tau = pl.pallas_call(
        _route_kernel,
        grid=(t // tl,),
        in_specs=[hs, hs],
        out_specs=[hs, hs, pl.BlockSpec((PEER_HEADS, 1, tl), lambda i: (0, 0, i))],
        out_shape=[score_shape, score_shape, jax.ShapeDtypeStruct((PEER_HEADS, 1, t), F32)],
        scratch_shapes=[pltpu.VMEM((PEER_TOPK, tl), F32)] * 3,
        compiler_params=_params("parallel"),
        name="route",
    )(s1t, s2t)

    u_bf = peer_expert_u[0].astype(BF16)
    vt_bf = peer_expert_v[0].T.astype(BF16)
    tt, te = TT_EXP, TE_EXP
    rt = pl.BlockSpec((PEER_HEADS, PEER_NKEYS, tt), lambda i, e: (0, 0, i))
    n_tiles = n_exp // te
    out = pl.pallas_call(
        _expert_kernel,
        grid=(t // tt, n_tiles + 1),
        in_specs=[pl.BlockSpec((tt, d), lambda i, e: (i, 0)),
                  pl.BlockSpec((te, d), lambda i, e: (jnp.minimum(e, n_tiles - 1), 0)),
                  pl.BlockSpec((d, te), lambda i, e: (0, jnp.maximum(e - 1, 0))),
                  rt, rt, rt, rt,
                  pl.BlockSpec((PEER_HEADS, 1, tt), lambda i, e: (0, 0, i)),
                  pl.BlockSpec((tt, d), lambda i, e: (i, 0)),
                  pl.BlockSpec((1, d), lambda i, e: (0, 0))],
        out_specs=pl.BlockSpec((tt, d), lambda i, e: (i, 0)),
        out_shape=jax.ShapeDtypeStruct((t, d), F32),
        scratch_shapes=[pltpu.VMEM((d, tt), F32), pltpu.VMEM((te, tt), F32),
                        pltpu.VMEM((te, tt), F32), pltpu.VMEM((te, tt), BF16)],
        compiler_params=_params("parallel", "arbitrary"),
        name="experts",
    )(h2, u_bf, vt_bf, s1t, s2t, a_co, e2, tau, x1, row(final_norm_g))
    return out.reshape(bsz, seq, d)
```

```python
import math

import jax
import jax.numpy as jnp
from jax import lax
from jax.experimental import pallas as pl
from jax.experimental.pallas import tpu as pltpu

F32 = jnp.float32
BF16 = jnp.bfloat16

D_MODEL = 1024
A_HEADS = 8
A_HEAD_DIM = 64
A_WIDTH = A_HEADS * A_HEAD_DIM
MOBA_BLOCK = 256
MOBA_TOPK = 3
B_HEADS = 8
B_NOPE = 64
B_ROPE = 32
B_V = 64
Q_LORA = 256
KV_LORA = 128
PEER_HEADS = 8
PEER_NKEYS = 128
PEER_HALF = 128
PEER_TOPK = 16
ROPE_THETA = 10000.0
RMS_EPS = 1e-6

LANES = 128
NEG_BIG = -1e30
LOG2E = math.log2(math.e)
VMEM_LIMIT = 56 * 1024 * 1024

TM_PROJ = 256
TK = MOBA_BLOCK
TQA = 8 * TK
TL_ROUTE = 256
TT_EXP = 512
TE_EXP = 512
TB_EXP = 512
TJ_EXP = 16


def _dot(a, b):
    return jnp.dot(a, b, preferred_element_type=F32)


def _dot_nt(a, b):
    return lax.dot_general(a, b, (((1,), (1,)), ((), ())), preferred_element_type=F32)


def _rms(x, g):
    return x * lax.rsqrt(jnp.mean(x * x, axis=-1, keepdims=True) + RMS_EPS) * g


def _inproj_kernel(x_ref, g_ref, wqkv_ref, wc_ref, wg_ref, wqup_ref, wkvup_ref, qng_ref,
                   kvng_ref, ca_ref, sa1_ref, sa2_ref, cb_ref, sb1_ref, sb2_ref,
                   qa_ref, ka_ref, vat_ref, qb_ref, kvb_ref, kvbt_ref, kpe_ref, sg_ref):
    h = _rms(x_ref[...], g_ref[...]).astype(BF16)

    qkv = _dot(h, wqkv_ref[...])
    ca, sa1, sa2 = ca_ref[...], sa1_ref[...], sa2_ref[...]

    def rope_a(xg):
        return xg * ca + pltpu.roll(xg, LANES - 32, 1) * sa1 + pltpu.roll(xg, 32, 1) * sa2

    for grp in range(A_WIDTH // LANES):
        lo = grp * LANES
        qa_ref[:, lo:lo + LANES] = (rope_a(qkv[:, lo:lo + LANES]) * (0.125 * LOG2E)).astype(BF16)
        ka_ref[:, lo:lo + LANES] = rope_a(qkv[:, A_WIDTH + lo:A_WIDTH + lo + LANES]).astype(BF16)
    vat_ref[...] = qkv[:, 2 * A_WIDTH:].T.astype(BF16)

    c = _dot(h, wc_ref[...])
    cb, sb1, sb2 = cb_ref[...], sb1_ref[...], sb2_ref[...]

    def rope_b(xg):
        return xg * cb + pltpu.roll(xg, LANES - 16, 1) * sb1 + pltpu.roll(xg, 16, 1) * sb2

    cq = _rms(c[:, :Q_LORA], qng_ref[...]).astype(BF16)
    qb = _dot(cq, wqup_ref[...]) * (LOG2E / math.sqrt(B_NOPE + B_ROPE))
    for hd in range(B_HEADS):
        lo = hd * LANES
        qb_ref[:, lo:lo + LANES] = rope_b(qb[:, lo:lo + LANES]).astype(BF16)
    ckv = _rms(c[:, Q_LORA:Q_LORA + KV_LORA], kvng_ref[...]).astype(BF16)
    kv = _dot(ckv, wkvup_ref[...])
    kvb_ref[...] = kv.astype(BF16)
    kvbt_ref[...] = kv.T.astype(BF16)
    kpe_ref[...] = rope_b(c[:, Q_LORA + KV_LORA:]).astype(BF16)

    sg_ref[...] = jax.nn.sigmoid(_dot(h, wg_ref[...])).astype(BF16)


def _attn_step(scores, carry, acc_ref, vts, c0=0):
    new, ps, alphas = [], [], []
    for s, (m, l) in zip(scores, carry):
        m_new = jnp.maximum(m[:, c0:], jnp.max(s, axis=0, keepdims=True))
        alpha = jnp.exp2(m[:, c0:] - m_new)
        p = jnp.exp2(s - m_new)
        l_new = alpha * l[:, c0:] + jnp.sum(p, axis=0, keepdims=True)
        if c0:
            m_new = jnp.concatenate([m[:, :c0], m_new], axis=1)
            l_new = jnp.concatenate([l[:, :c0], l_new], axis=1)
        new.append((m_new, l_new))
        ps.append(p.astype(BF16))
        alphas.append(alpha)
    for hh in range(2):
        acc_ref[hh, :, c0:] = alphas[hh] * acc_ref[hh, :, c0:] + _dot(vts[hh], ps[hh])
    return tuple(new)


def _attn_loop(i, step):
    per = TQA // TK
    init = tuple((jnp.full((1, TQA), NEG_BIG, F32), jnp.zeros((1, TQA), F32)) for _ in range(2))
    carry = lax.fori_loop(0, per * i, lambda j, c: step(j, c, None, 0), init)
    for a in range(per):
        c0 = a * TK
        krow = lax.broadcasted_iota(jnp.int32, (TK, TQA - c0), 0)
        qcol = lax.broadcasted_iota(jnp.int32, (TK, TQA - c0), 1)
        carry = step(per * i + a, carry, krow <= qcol, c0)
    return carry


def _transpose_bf16(x):
    return x.astype(F32).T.astype(BF16)


def _moba_kernel(q_ref, k_ref, vt_ref, o_ref, kmh_ref, kml_ref, acc_ref):
    i = pl.program_id(2)
    nblk = k_ref.shape[0] // TK

    @pl.when(i == 0)
    def _():
        rows = lax.broadcasted_iota(jnp.int32, (LANES, LANES), 0)
        km = jnp.zeros((LANES, LANES), F32)
        for j in range(nblk):
            blk = k_ref[j * TK:(j + 1) * TK, :].astype(F32)
            km = jnp.where(rows == j, jnp.mean(blk, axis=0, keepdims=True), km)
        hi = km.astype(BF16)
        kmh_ref[...] = hi
        kml_ref[...] = (km - hi.astype(F32)).astype(BF16)

    qt = _transpose_bf16(q_ref[...])
    sub = lax.broadcasted_iota(jnp.int32, (LANES, TQA), 0)
    col = lax.broadcasted_iota(jnp.int32, (LANES, TQA), 1)
    own = (TQA // TK) * i + col // TK
    zero = jnp.zeros_like(qt)
    qt_heads = (jnp.where(sub < A_HEAD_DIM, qt, zero), jnp.where(sub >= A_HEAD_DIM, qt, zero))

    q_aug = []
    for qh in qt_heads:
        gate = _dot(kmh_ref[...], qh) + _dot(kml_ref[...], qh)
        valid = sub < own
        g = jnp.where(valid, gate, -jnp.inf)[:nblk]
        thr = jnp.max(g, axis=0, keepdims=True)
        for _ in range(MOBA_TOPK - 1):
            g = jnp.where(g >= thr, -jnp.inf, g)
            thr = jnp.max(g, axis=0, keepdims=True)
        keep = (valid & (gate >= thr)) | (sub == own)
        bias = jnp.where(keep, 0.0, NEG_BIG).astype(BF16)
        q_aug.append(jnp.concatenate([qh, bias], axis=0))

    lane = lax.broadcasted_iota(jnp.int32, (TK, LANES), 1)
    acc_ref[...] = jnp.zeros_like(acc_ref)

    def step(j, carry, causal, c0):
        start = pl.multiple_of(j * TK, TK)
        onehot = jnp.where(lane == j, 1.0, 0.0).astype(BF16)
        ka = jnp.concatenate([k_ref[pl.ds(start, TK), :], onehot], axis=1)
        scores = [_dot(ka, qa[:, c0:]) for qa in q_aug]
        if causal is not None:
            scores = [jnp.where(causal, s, NEG_BIG) for s in scores]
        vt = vt_ref[j]
        return _attn_step(scores, carry, acc_ref, (vt, vt), c0)

    carry = _attn_loop(i, step)
    outs = [acc_ref[hh] / carry[hh][1] for hh in range(2)]
    o_ref[...] = jnp.where(sub < A_HEAD_DIM, outs[0], outs[1]).T.astype(o_ref.dtype)


def _mla_kernel(q_ref, kv_ref, kpe_ref, kvt_ref, o_ref, acc_ref):
    i = pl.program_id(2)
    lane = lax.broadcasted_iota(jnp.int32, (TK, LANES), 1)
    qts = (_transpose_bf16(q_ref[:, :LANES]), _transpose_bf16(q_ref[:, LANES:]))
    acc_ref[...] = jnp.zeros_like(acc_ref)

    def step(j, carry, causal, c0):
        start = pl.multiple_of(j * TK, TK)
        kpe = kpe_ref[pl.ds(start, TK), :]
        scores = []
        for hh in range(2):
            kv = kv_ref[pl.ds(start, TK), hh * LANES:(hh + 1) * LANES]
            key = jnp.where(lane < B_NOPE, kv, kpe)
            scores.append(_dot(key, qts[hh][:, c0:]))
        if causal is not None:
            scores = [jnp.where(causal, s, NEG_BIG) for s in scores]
        return _attn_step(scores, carry, acc_ref,
                          (kvt_ref[j, :LANES, :], kvt_ref[j, LANES:, :]), c0)

    carry = _attn_loop(i, step)
    outs = [(acc_ref[hh] / carry[hh][1])[B_NOPE:, :] for hh in range(2)]
    o_ref[...] = jnp.concatenate(outs, axis=0).T.astype(o_ref.dtype)


def _merge_kernel(x_ref, ya_ref, yb_ref, sg_ref, wba_ref, wbb_ref, wout_ref, fg_ref, wpq_ref,
                  sk1_ref, sk2_ref, x1_ref, h2_ref, s1_ref, s2_ref):
    merged = (sg_ref[:, :D_MODEL].astype(F32) * _dot(ya_ref[...], wba_ref[...])
              + sg_ref[:, D_MODEL:].astype(F32) * _dot(yb_ref[...], wbb_ref[...]))
    x1 = x_ref[...] + _dot(merged.astype(BF16), wout_ref[...])
    x1_ref[...] = x1
    h2 = _rms(x1, fg_ref[...]).astype(BF16)
    h2_ref[...] = h2
    qp = _dot(h2, wpq_ref[...]).astype(BF16)
    for hd in range(PEER_HEADS):
        lo = 2 * hd * PEER_HALF
        s1_ref[hd] = _dot_nt(sk1_ref[hd], qp[:, lo:lo + PEER_HALF])
        s2_ref[hd] = _dot_nt(sk2_ref[hd], qp[:, lo + PEER_HALF:lo + 2 * PEER_HALF])


def _route_kernel(s1_ref, s2_ref, a_ref, e2_ref, tau_ref, v1_ref, v2_ref, best_ref):
    def top16(src, dst_ref):
        cur = src
        for r in range(PEER_TOPK):
            mx = jnp.max(cur, axis=0, keepdims=True)
            dst_ref[r:r + 1, :] = mx
            cur = jnp.where(cur == mx, -jnp.inf, cur)

    def head(hd, carry):
        s1 = s1_ref[hd]
        s2 = s2_ref[hd]
        top16(s1, v1_ref)
        top16(s2, v2_ref)
        v1 = v1_ref[...]
        v2 = v2_ref[...]
        half = PEER_TOPK // 2
        cand = jnp.concatenate(
            [v1 + v2[0:1]] + [v1[:half] + v2[r:r + 1] for r in range(1, half)]
            + [v1[0:1] + v2[half:]], axis=0)
        top16(cand, best_ref)
        best = best_ref[...]
        z = jnp.sum(jnp.exp(best - best[0:1, :]), axis=0, keepdims=True)
        tau_ref[hd] = best[PEER_TOPK - 1:PEER_TOPK, :]
        a_ref[hd] = jnp.exp(s1 - v1_ref[0:1, :]) / z
        e2_ref[hd] = jnp.exp(s2 - v2_ref[0:1, :])
        return carry

    lax.fori_loop(0, PEER_HEADS, head, 0)


def _expert_kernel(h2_ref, u_ref, vt_ref, s1_ref, s2_ref, a_ref, e2_ref, tau_ref, x1_ref, fg_ref,
                   o_ref, acc_ref, act0_ref, act1_ref, wa_ref):
    e = pl.program_id(1)
    n_tiles = pl.num_programs(1) - 1
    slabs = TE_EXP // PEER_NKEYS
    prev = jnp.maximum(e - 1, 0)

    @pl.when(e == 0)
    def _():
        acc_ref[...] = jnp.zeros_like(acc_ref)
        act1_ref[...] = jnp.zeros_like(act1_ref)

    def body(act_w_ref, act_r_ref):
        for tb in range(TT_EXP // TB_EXP):
            ts = slice(tb * TB_EXP, (tb + 1) * TB_EXP)
            pre = _dot_nt(u_ref[...], h2_ref[ts, :])
            act_w_ref[:, ts] = 0.5 * pre * (1.0 + lax.erf(pre * math.sqrt(0.5)))
            tau = [tau_ref[hd, :, ts] for hd in range(PEER_HEADS)]
            for sl in range(slabs):
                i1 = prev * slabs + sl
                s1r = [s1_ref[hd, pl.ds(i1, 1), ts] for hd in range(PEER_HEADS)]
                ar = [a_ref[hd, pl.ds(i1, 1), ts] for hd in range(PEER_HEADS)]
                for jt in range(PEER_NKEYS // TJ_EXP):
                    rj = slice(jt * TJ_EXP, (jt + 1) * TJ_EXP)
                    w = None
                    for hd in range(PEER_HEADS):
                        cand = s2_ref[hd, rj, ts] + s1r[hd]
                        term = jnp.where(cand >= tau[hd], e2_ref[hd, rj, ts], 0.0) * ar[hd]
                        w = term if w is None else w + term
                    ro = slice(sl * PEER_NKEYS + jt * TJ_EXP, sl * PEER_NKEYS + (jt + 1) * TJ_EXP)
                    wa_ref[ro, ts] = (w * act_r_ref[ro, ts]).astype(BF16)
            acc_ref[:, ts] += _dot(vt_ref[...], wa_ref[:, ts])

    @pl.when(e % 2 == 0)
    def _():
        body(act0_ref, act1_ref)

    @pl.when(e % 2 == 1)
    def _():
        body(act1_ref, act0_ref)

    @pl.when(e == n_tiles)
    def _():
        o_ref[...] = _rms(x1_ref[...] + acc_ref[...].T, fg_ref[...])


def _params(*sem):
    return pltpu.CompilerParams(dimension_semantics=sem, vmem_limit_bytes=VMEM_LIMIT)


def _full(shape):
    nd = len(shape)
    return pl.BlockSpec(shape, lambda *_: (0,) * nd)


def _rope_tables(positions):
    pos = positions.reshape(-1).astype(F32)[:, None]

    def cs(dim):
        inv_freq = ROPE_THETA ** (-jnp.arange(0, dim, 2, dtype=F32) / dim)
        ang = pos * inv_freq
        return jnp.cos(ang), jnp.sin(ang)

    ca, sa = cs(A_HEAD_DIM)
    za = jnp.zeros_like(sa)
    cb, sb = cs(B_ROPE)
    t = pos.shape[0]
    zb = lambda n: jnp.zeros((t, n), F32)
    ob = lambda n: jnp.ones((t, n), F32)
    return (jnp.concatenate([ca] * 4, axis=1),
            jnp.concatenate([-sa, za, -sa, za], axis=1),
            jnp.concatenate([za, sa, za, sa], axis=1),
            jnp.concatenate([ob(B_NOPE), cb, cb, ob(32)], axis=1),
            jnp.concatenate([zb(B_NOPE), -sb, zb(16), zb(32)], axis=1),
            jnp.concatenate([zb(B_NOPE), zb(16), sb, zb(32)], axis=1))


def kernel(x, positions, mix_norm_g, w_in, q_norm_g, w_q_up, kv_norm_g, w_kv_up, w_branch_a,
           w_branch_b, w_out, ffn_norm_g, w_peer_query, peer_sub_keys_1, peer_sub_keys_2,
           peer_expert_u, peer_expert_v, final_norm_g):
    bsz, seq, d = x.shape
    t = bsz * seq
    n_exp = peer_expert_u.shape[1]
    assert d == D_MODEL and seq % TQA == 0 and t % TT_EXP == 0 and n_exp % TE_EXP == 0
    assert mix_norm_g.shape[0] == 1, "single-layer block"

    w = w_in[0]
    o_cq = 3 * A_WIDTH
    o_ckv = o_cq + Q_LORA
    o_kpe = o_ckv + KV_LORA
    o_g = o_kpe + B_ROPE
    wqkv = w[:, :o_cq].astype(BF16)
    wc = jnp.concatenate([w[:, o_cq:o_kpe], jnp.zeros((d, B_NOPE), F32), w[:, o_kpe:o_g],
                          jnp.zeros((d, LANES - B_NOPE - B_ROPE), F32)], axis=1).astype(BF16)
    wg = w[:, o_g:].astype(BF16)
    wqup = jnp.pad(w_q_up[0].reshape(Q_LORA, B_HEADS, B_NOPE + B_ROPE),
                   ((0, 0), (0, 0), (0, LANES - B_NOPE - B_ROPE))).reshape(Q_LORA, B_HEADS * LANES)
    wqup = wqup.astype(BF16)
    wkvup = w_kv_up[0].astype(BF16)
    row = lambda v: v.reshape(1, -1).astype(F32)
    tables = _rope_tables(positions)
    x2d = x.reshape(t, d)

    tm = TM_PROJ
    assert tm == TK, "one projection tile is one attention key block (transposed value tiles)"
    tok = lambda width: pl.BlockSpec((tm, width), lambda i: (i, 0))
    bf = lambda width: jax.ShapeDtypeStruct((t, width), BF16)
    tok_t = lambda width: pl.BlockSpec((None, width, tm), lambda i: (i, 0, 0))
    bf_t = lambda width: jax.ShapeDtypeStruct((t // tm, width, tm), BF16)
    qa, ka, vat, qb, kvb, kvbt, kpe, sg = pl.pallas_call(
        _inproj_kernel,
        grid=(t // tm,),
        in_specs=[tok(d), _full((1, d)), _full(wqkv.shape), _full(wc.shape), _full(wg.shape),
                  _full(wqup.shape), _full(wkvup.shape), _full((1, Q_LORA)), _full((1, KV_LORA))]
                 + [tok(LANES)] * 6,
        out_specs=[tok(A_WIDTH), tok(A_WIDTH), tok_t(A_WIDTH), tok(B_HEADS * LANES),
                   tok(B_HEADS * LANES), tok_t(B_HEADS * LANES), tok(LANES), tok(2 * d)],
        out_shape=[bf(A_WIDTH), bf(A_WIDTH), bf_t(A_WIDTH), bf(B_HEADS * LANES),
                   bf(B_HEADS * LANES), bf_t(B_HEADS * LANES), bf(LANES), bf(2 * d)],
        compiler_params=_params("parallel"),
        name="inproj",
    )(x2d, row(mix_norm_g[0]), wqkv, wc, wg, wqup, wkvup, row(q_norm_g[0]), row(kv_norm_g[0]),
      *tables)

    r3 = lambda a: a.reshape(bsz, seq, a.shape[-1])
    r4t = lambda a: a.reshape(bsz, seq // tm, a.shape[1], tm)
    nblk = seq // TK
    assert nblk <= LANES
    pairs = A_WIDTH // LANES
    ya = pl.pallas_call(
        _moba_kernel,
        grid=(bsz, pairs, seq // TQA),
        in_specs=[pl.BlockSpec((None, TQA, LANES), lambda b, g, i: (b, i, g)),
                  pl.BlockSpec((None, seq, LANES), lambda b, g, i: (b, 0, g)),
                  pl.BlockSpec((None, nblk, LANES, TK), lambda b, g, i: (b, 0, g, 0))],
        out_specs=pl.BlockSpec((None, TQA, LANES), lambda b, g, i: (b, i, g)),
        out_shape=jax.ShapeDtypeStruct((bsz, seq, A_WIDTH), BF16),
        scratch_shapes=[pltpu.VMEM((LANES, LANES), BF16), pltpu.VMEM((LANES, LANES), BF16),
                        pltpu.VMEM((2, LANES, TQA), F32)],
        compiler_params=_params("parallel", "parallel", "arbitrary"),
        name="moba",
    )(r3(qa), r3(ka), r4t(vat))

    yb = pl.pallas_call(
        _mla_kernel,
        grid=(bsz, B_HEADS // 2, seq // TQA),
        in_specs=[pl.BlockSpec((None, TQA, 2 * LANES), lambda b, g, i: (b, i, g)),
                  pl.BlockSpec((None, seq, 2 * LANES), lambda b, g, i: (b, 0, g)),
                  pl.BlockSpec((None, seq, LANES), lambda b, g, i: (b, 0, 0)),
                  pl.BlockSpec((None, nblk, 2 * LANES, TK), lambda b, g, i: (b, 0, g, 0))],
        out_specs=pl.BlockSpec((None, TQA, LANES), lambda b, g, i: (b, i, g)),
        out_shape=jax.ShapeDtypeStruct((bsz, seq, B_HEADS * B_V), BF16),
        scratch_shapes=[pltpu.VMEM((2, LANES, TQA), F32)],
        compiler_params=_params("parallel", "parallel", "arbitrary"),
        name="mla",
    )(r3(qb), r3(kvb), r3(kpe), r4t(kvbt))

    wba = w_branch_a[0].astype(BF16)
    wbb = w_branch_b[0].astype(BF16)
    wout = w_out[0].astype(BF16)
    wpq = w_peer_query[0].astype(BF16)
    sk1 = peer_sub_keys_1[0].astype(BF16)
    sk2 = peer_sub_keys_2[0].astype(BF16)
    score_spec = pl.BlockSpec((PEER_HEADS, PEER_NKEYS, tm), lambda i: (0, 0, i))
    score_shape = jax.ShapeDtypeStruct((PEER_HEADS, PEER_NKEYS, t), F32)
    x1, h2, s1t, s2t = pl.pallas_call(
        _merge_kernel,
        grid=(t // tm,),
        in_specs=[tok(d), tok(A_WIDTH), tok(B_HEADS * B_V), tok(2 * d), _full(wba.shape),
                  _full(wbb.shape), _full(wout.shape), _full((1, d)), _full(wpq.shape),
                  _full(sk1.shape), _full(sk2.shape)],
        out_specs=[tok(d), tok(d), score_spec, score_spec],
        out_shape=[jax.ShapeDtypeStruct((t, d), F32), bf(d), score_shape, score_shape],
        compiler_params=_params("parallel"),
        name="merge",
    )(x2d, ya.reshape(t, A_WIDTH), yb.reshape(t, B_HEADS * B_V), sg, wba, wbb, wout,
      row(ffn_norm_g[0]), wpq, sk1, sk2)

    tl = TL_ROUTE
    hs = pl.BlockSpec((PEER_HEADS, PEER_NKEYS, tl), lambda i: (0, 0, i))
    a_co, e2, tau = pl.pallas_call(
        _route_kernel,
        grid=(t // tl,),
        in_specs=[hs, hs],
        out_specs=[hs, hs, pl.BlockSpec((PEER_HEADS, 1, tl), lambda i: (0, 0, i))],
        out_shape=[score_shape, score_shape, jax.ShapeDtypeStruct((PEER_HEADS, 1, t), F32)],
        scratch_shapes=[pltpu.VMEM((PEER_TOPK, tl), F32)] * 3,
        compiler_params=_params("parallel"),
        name="route",
    )(s1t, s2t)

    u_bf = peer_expert_u[0].astype(BF16)
    vt_bf = peer_expert_v[0].T.astype(BF16)
    tt, te = TT_EXP, TE_EXP
    rt = pl.BlockSpec((PEER_HEADS, PEER_NKEYS, tt), lambda i, e: (0, 0, i))
    n_tiles = n_exp // te
    out = pl.pallas_call(
        _expert_kernel,
        grid=(t // tt, n_tiles + 1),
        in_specs=[pl.BlockSpec((tt, d), lambda i, e: (i, 0)),
                  pl.BlockSpec((te, d), lambda i, e: (jnp.minimum(e, n_tiles - 1), 0)),
                  pl.BlockSpec((d, te), lambda i, e: (0, jnp.maximum(e - 1, 0))),
                  rt, rt, rt, rt,
                  pl.BlockSpec((PEER_HEADS, 1, tt), lambda i, e: (0, 0, i)),
                  pl.BlockSpec((tt, d), lambda i, e: (i, 0)),
                  pl.BlockSpec((1, d), lambda i, e: (0, 0))],
        out_specs=pl.BlockSpec((tt, d), lambda i, e: (i, 0)),
        out_shape=jax.ShapeDtypeStruct((t, d), F32),
        scratch_shapes=[pltpu.VMEM((d, tt), F32), pltpu.VMEM((te, tt), F32),
                        pltpu.VMEM((te, tt), F32), pltpu.VMEM((te, tt), BF16)],
        compiler_params=_params("parallel", "arbitrary"),
        name="experts",
    )(h2, u_bf, vt_bf, s1t, s2t, a_co, e2, tau, x1, row(final_norm_g))
    return out.reshape(bsz, seq, d)
```

```python
import math

import jax
import jax.numpy as jnp
from jax import lax
from jax.experimental import pallas as pl
from jax.experimental.pallas import tpu as pltpu

F32 = jnp.float32
BF16 = jnp.bfloat16

D_MODEL = 1024
A_HEADS = 8
A_HEAD_DIM = 64
A_WIDTH = A_HEADS * A_HEAD_DIM
MOBA_BLOCK = 256
MOBA_TOPK = 3
B_HEADS = 8
B_NOPE = 64
B_ROPE = 32
B_V = 64
Q_LORA = 256
KV_LORA = 128
PEER_HEADS = 8
PEER_NKEYS = 128
PEER_HALF = 128
PEER_TOPK = 16
ROPE_THETA = 10000.0
RMS_EPS = 1e-6

LANES = 128
SUBLANES = 8
NEG_BIG = -1e30
LOG2E = math.log2(math.e)
VMEM_LIMIT = 56 * 1024 * 1024

TM_PROJ = 256
TK = MOBA_BLOCK
TQA = 8 * TK
TL_ROUTE = 256
TT_EXP = 512
TE_EXP = 512
TB_EXP = 256
TJ_EXP = 16


def _dot(a, b):
    return jnp.dot(a, b, preferred_element_type=F32)


def _dot_nt(a, b):
    return lax.dot_general(a, b, (((1,), (1,)), ((), ())), preferred_element_type=F32)


def _rms(x, g):
    return x * lax.rsqrt(jnp.mean(x * x, axis=-1, keepdims=True) + RMS_EPS) * g


def _inproj_kernel(x_ref, g_ref, wqkv_ref, wc_ref, wg_ref, wqup_ref, wkvup_ref, qng_ref,
                   kvng_ref, ca_ref, sa1_ref, sa2_ref, cb_ref, sb1_ref, sb2_ref,
                   qa_ref, ka_ref, vat_ref, qb_ref, kvb_ref, kvbt_ref, kpe_ref, sg_ref):
    h = _rms(x_ref[...], g_ref[...]).astype(BF16)

    qkv = _dot(h, wqkv_ref[...])
    ca, sa1, sa2 = ca_ref[...], sa1_ref[...], sa2_ref[...]

    def rope_a(xg):
        return xg * ca + pltpu.roll(xg, LANES - 32, 1) * sa1 + pltpu.roll(xg, 32, 1) * sa2

    for grp in range(A_WIDTH // LANES):
        lo = grp * LANES
        qa_ref[:, lo:lo + LANES] = (rope_a(qkv[:, lo:lo + LANES]) * (0.125 * LOG2E)).astype(BF16)
        ka_ref[:, lo:lo + LANES] = rope_a(qkv[:, A_WIDTH + lo:A_WIDTH + lo + LANES]).astype(BF16)
    vat_ref[...] = qkv[:, 2 * A_WIDTH:].T.astype(BF16)

    c = _dot(h, wc_ref[...])
    cb, sb1, sb2 = cb_ref[...], sb1_ref[...], sb2_ref[...]

    def rope_b(xg):
        return xg * cb + pltpu.roll(xg, LANES - 16, 1) * sb1 + pltpu.roll(xg, 16, 1) * sb2

    cq = _rms(c[:, :Q_LORA], qng_ref[...]).astype(BF16)
    qb = _dot(cq, wqup_ref[...]) * (LOG2E / math.sqrt(B_NOPE + B_ROPE))
    for hd in range(B_HEADS):
        lo = hd * LANES
        qb_ref[:, lo:lo + LANES] = rope_b(qb[:, lo:lo + LANES]).astype(BF16)
    ckv = _rms(c[:, Q_LORA:Q_LORA + KV_LORA], kvng_ref[...]).astype(BF16)
    kv = _dot(ckv, wkvup_ref[...])
    kvb_ref[...] = kv.astype(BF16)
    kvbt_ref[...] = kv.T.astype(BF16)
    kpe_ref[...] = rope_b(c[:, Q_LORA + KV_LORA:]).astype(BF16)

    sg_ref[...] = jax.nn.sigmoid(_dot(h, wg_ref[...])).astype(BF16)


def _attn_step(scores, carry, acc_ref, vts, c0=0):
    new, ps, alphas = [], [], []
    for s, (m, l) in zip(scores, carry):
        m_new = jnp.maximum(m[:, c0:], jnp.max(s, axis=0, keepdims=True))
        alpha = jnp.exp2(m[:, c0:] - m_new)
        p = jnp.exp2(s - m_new)
        l_new = alpha * l[:, c0:] + jnp.sum(p, axis=0, keepdims=True)
        if c0:
            m_new = jnp.concatenate([m[:, :c0], m_new], axis=1)
            l_new = jnp.concatenate([l[:, :c0], l_new], axis=1)
        new.append((m_new, l_new))
        ps.append(p.astype(BF16))
        alphas.append(alpha)
    for hh in range(2):
        acc_ref[hh, :, c0:] = alphas[hh] * acc_ref[hh, :, c0:] + _dot(vts[hh], ps[hh])
    return tuple(new)


def _attn_loop(i, step):
    per = TQA // TK
    init = tuple((jnp.full((1, TQA), NEG_BIG, F32), jnp.zeros((1, TQA), F32)) for _ in range(2))
    carry = lax.fori_loop(0, per * i, lambda j, c: step(j, c, None, 0), init)
    for a in range(per):
        c0 = a * TK
        krow = lax.broadcasted_iota(jnp.int32, (TK, TQA - c0), 0)
        qcol = lax.broadcasted_iota(jnp.int32, (TK, TQA - c0), 1)
        carry = step(per * i + a, carry, krow <= qcol, c0)
    return carry


def _transpose_bf16(x):
    return x.astype(F32).T.astype(BF16)


def _moba_kernel(q_ref, k_ref, vt_ref, o_ref, kmh_ref, kml_ref, acc_ref):
    i = pl.program_id(2)
    nblk = k_ref.shape[0] // TK

    @pl.when(i == 0)
    def _():
        rows = lax.broadcasted_iota(jnp.int32, (LANES, LANES), 0)
        km = jnp.zeros((LANES, LANES), F32)
        for j in range(nblk):
            blk = k_ref[j * TK:(j + 1) * TK, :].astype(F32)
            km = jnp.where(rows == j, jnp.mean(blk, axis=0, keepdims=True), km)
        hi = km.astype(BF16)
        kmh_ref[...] = hi
        kml_ref[...] = (km - hi.astype(F32)).astype(BF16)

    qt = _transpose_bf16(q_ref[...])
    sub = lax.broadcasted_iota(jnp.int32, (LANES, TQA), 0)
    col = lax.broadcasted_iota(jnp.int32, (LANES, TQA), 1)
    own = (TQA // TK) * i + col // TK
    zero = jnp.zeros_like(qt)
    qt_heads = (jnp.where(sub < A_HEAD_DIM, qt, zero), jnp.where(sub >= A_HEAD_DIM, qt, zero))

    q_aug = []
    for qh in qt_heads:
        gate = _dot(kmh_ref[...], qh) + _dot(kml_ref[...], qh)
        valid = sub < own
        g = jnp.where(valid, gate, -jnp.inf)[:nblk]
        thr = jnp.max(g, axis=0, keepdims=True)
        for _ in range(MOBA_TOPK - 1):
            g = jnp.where(g >= thr, -jnp.inf, g)
            thr = jnp.max(g, axis=0, keepdims=True)
        keep = (valid & (gate >= thr)) | (sub == own)
        bias = jnp.where(keep, 0.0, NEG_BIG).astype(BF16)
        q_aug.append(jnp.concatenate([qh, bias], axis=0))

    lane = lax.broadcasted_iota(jnp.int32, (TK, LANES), 1)
    acc_ref[...] = jnp.zeros_like(acc_ref)

    def step(j, carry, causal, c0):
        start = pl.multiple_of(j * TK, TK)
        onehot = jnp.where(lane == j, 1.0, 0.0).astype(BF16)
        ka = jnp.concatenate([k_ref[pl.ds(start, TK), :], onehot], axis=1)
        scores = [_dot(ka, qa[:, c0:]) for qa in q_aug]
        if causal is not None:
            scores = [jnp.where(causal, s, NEG_BIG) for s in scores]
        vt = vt_ref[j]
        return _attn_step(scores, carry, acc_ref, (vt, vt), c0)

    carry = _attn_loop(i, step)
    outs = [acc_ref[hh] / carry[hh][1] for hh in range(2)]
    o_ref[...] = jnp.where(sub < A_HEAD_DIM, outs[0], outs[1]).T.astype(o_ref.dtype)


def _mla_kernel(q_ref, kv_ref, kpe_ref, kvt_ref, o_ref, acc_ref):
    i = pl.program_id(2)
    lane = lax.broadcasted_iota(jnp.int32, (TK, LANES), 1)
    qts = (_transpose_bf16(q_ref[:, :LANES]), _transpose_bf16(q_ref[:, LANES:]))
    acc_ref[...] = jnp.zeros_like(acc_ref)

    def step(j, carry, causal, c0):
        start = pl.multiple_of(j * TK, TK)
        kpe = kpe_ref[pl.ds(start, TK), :]
        scores = []
        for hh in range(2):
            kv = kv_ref[pl.ds(start, TK), hh * LANES:(hh + 1) * LANES]
            key = jnp.where(lane < B_NOPE, kv, kpe)
            scores.append(_dot(key, qts[hh][:, c0:]))
        if causal is not None:
            scores = [jnp.where(causal, s, NEG_BIG) for s in scores]
        return _attn_step(scores, carry, acc_ref,
                          (kvt_ref[j, :LANES, :], kvt_ref[j, LANES:, :]), c0)

    carry = _attn_loop(i, step)
    outs = [(acc_ref[hh] / carry[hh][1])[B_NOPE:, :] for hh in range(2)]
    o_ref[...] = jnp.concatenate(outs, axis=0).T.astype(o_ref.dtype)


def _merge_kernel(x_ref, ya_ref, yb_ref, sg_ref, wba_ref, wbb_ref, wout_ref, fg_ref, wpq_ref,
                  sk1_ref, sk2_ref, x1_ref, h2_ref, s1_ref, s2_ref):
    merged = (sg_ref[:, :D_MODEL].astype(F32) * _dot(ya_ref[...], wba_ref[...])
              + sg_ref[:, D_MODEL:].astype(F32) * _dot(yb_ref[...], wbb_ref[...]))
    x1 = x_ref[...] + _dot(merged.astype(BF16), wout_ref[...])
    x1_ref[...] = x1
    h2 = _rms(x1, fg_ref[...]).astype(BF16)
    h2_ref[...] = h2
    qp = _dot(h2, wpq_ref[...]).astype(BF16)
    for hd in range(PEER_HEADS):
        lo = 2 * hd * PEER_HALF
        s1_ref[hd] = _dot_nt(sk1_ref[hd], qp[:, lo:lo + PEER_HALF])
        s2_ref[hd] = _dot_nt(sk2_ref[hd], qp[:, lo + PEER_HALF:lo + 2 * PEER_HALF])


def _route_kernel(s1_ref, s2_ref, a_ref, e2_ref, tau_ref, v1_ref, v2_ref, best_ref):
    def top16(src, dst_ref):
        cur = src
        for r in range(PEER_TOPK):
            mx = jnp.max(cur, axis=0, keepdims=True)
            dst_ref[r:r + 1, :] = mx
            cur = jnp.where(cur == mx, -jnp.inf, cur)

    def head(hd, carry):
        s1 = s1_ref[hd]
        s2 = s2_ref[hd]
        top16(s1, v1_ref)
        top16(s2, v2_ref)
        v1 = v1_ref[...]
        v2 = v2_ref[...]
        half = PEER_TOPK // 2
        cand = jnp.concatenate(
            [v1 + v2[0:1]] + [v1[:half] + v2[r:r + 1] for r in range(1, half)]
            + [v1[0:1] + v2[half:]], axis=0)
        top16(cand, best_ref)
        best = best_ref[...]
        z = jnp.sum(jnp.exp(best - best[0:1, :]), axis=0, keepdims=True)
        tau_ref[hd] = best[PEER_TOPK - 1:PEER_TOPK, :]
        a_ref[hd] = jnp.exp(s1 - v1_ref[0:1, :]) / z
        e2_ref[hd] = jnp.exp(s2 - v2_ref[0:1, :])
        return carry

    lax.fori_loop(0, PEER_HEADS, head, 0)


def _expert_kernel(h2_ref, u_ref, vt_ref, s1_ref, s2_ref, a_ref, e2_ref, tau_ref, x1_ref, fg_ref,
                   o_ref, acc_ref, act0_ref, act1_ref, wa_ref):
    e = pl.program_id(1)
    n_tiles = pl.num_programs(1) - 1
    slabs = TE_EXP // PEER_NKEYS
    prev = jnp.maximum(e - 1, 0)

    @pl.when(e == 0)
    def _():
        acc_ref[...] = jnp.zeros_like(acc_ref)
        act1_ref[...] = jnp.zeros_like(act1_ref)

    def body(act_w_ref, act_r_ref):
        for tb in range(TT_EXP // TB_EXP):
            ts = slice(tb * TB_EXP, (tb + 1) * TB_EXP)
            pre = _dot_nt(u_ref[...], h2_ref[ts, :])
            act_w_ref[:, ts] = 0.5 * pre * (1.0 + lax.erf(pre * math.sqrt(0.5)))
            for lb in range(TB_EXP // LANES):
                tl = slice(tb * TB_EXP + lb * LANES, tb * TB_EXP + (lb + 1) * LANES)
                vreg = (SUBLANES, LANES)

                def bcast(row):
                    return jnp.broadcast_to(row[:, lb * LANES:(lb + 1) * LANES], vreg)

                tau = [bcast(tau_ref[hd, :, ts]) for hd in range(PEER_HEADS)]
                for sl in range(slabs):
                    i1 = prev * slabs + sl
                    s1r = [bcast(s1_ref[hd, pl.ds(i1, 1), ts]) for hd in range(PEER_HEADS)]
                    ar = [bcast(a_ref[hd, pl.ds(i1, 1), ts]) for hd in range(PEER_HEADS)]
                    for jt in range(PEER_NKEYS // TJ_EXP):
                        parts = []
                        for jh in range(TJ_EXP // SUBLANES):
                            j0 = jt * TJ_EXP + jh * SUBLANES
                            rj = slice(j0, j0 + SUBLANES)
                            w = None
                            for hd in range(PEER_HEADS):
                                cand = s2_ref[hd, rj, tl] + s1r[hd]
                                term = jnp.where(cand >= tau[hd], e2_ref[hd, rj, tl], 0.0) * ar[hd]
                                w = term if w is None else w + term
                            ro = slice(sl * PEER_NKEYS + j0, sl * PEER_NKEYS + j0 + SUBLANES)
                            parts.append(w * act_r_ref[ro, tl])
                        ro = slice(sl * PEER_NKEYS + jt * TJ_EXP, sl * PEER_NKEYS + (jt + 1) * TJ_EXP)
                        wa_ref[ro, tl] = jnp.concatenate(parts, axis=0).astype(BF16)
            acc_ref[:, ts] += _dot(vt_ref[...], wa_ref[:, ts])

    @pl.when(e % 2 == 0)
    def _():
        body(act0_ref, act1_ref)

    @pl.when(e % 2 == 1)
    def _():
        body(act1_ref, act0_ref)

    @pl.when(e == n_tiles)
    def _():
        o_ref[...] = _rms(x1_ref[...] + acc_ref[...].T, fg_ref[...])


def _params(*sem):
    return pltpu.CompilerParams(dimension_semantics=sem, vmem_limit_bytes=VMEM_LIMIT)


def _full(shape):
    nd = len(shape)
    return pl.BlockSpec(shape, lambda *_: (0,) * nd)


def _rope_tables(positions):
    pos = positions.reshape(-1).astype(F32)[:, None]

    def cs(dim):
        inv_freq = ROPE_THETA ** (-jnp.arange(0, dim, 2, dtype=F32) / dim)
        ang = pos * inv_freq
        return jnp.cos(ang), jnp.sin(ang)

    ca, sa = cs(A_HEAD_DIM)
    za = jnp.zeros_like(sa)
    cb, sb = cs(B_ROPE)
    t = pos.shape[0]
    zb = lambda n: jnp.zeros((t, n), F32)
    ob = lambda n: jnp.ones((t, n), F32)
    return (jnp.concatenate([ca] * 4, axis=1),
            jnp.concatenate([-sa, za, -sa, za], axis=1),
            jnp.concatenate([za, sa, za, sa], axis=1),
            jnp.concatenate([ob(B_NOPE), cb, cb, ob(32)], axis=1),
            jnp.concatenate([zb(B_NOPE), -sb, zb(16), zb(32)], axis=1),
            jnp.concatenate([zb(B_NOPE), zb(16), sb, zb(32)], axis=1))


def kernel(x, positions, mix_norm_g, w_in, q_norm_g, w_q_up, kv_norm_g, w_kv_up, w_branch_a,
           w_branch_b, w_out, ffn_norm_g, w_peer_query, peer_sub_keys_1, peer_sub_keys_2,
           peer_expert_u, peer_expert_v, final_norm_g):
    bsz, seq, d = x.shape
    t = bsz * seq
    n_exp = peer_expert_u.shape[1]
    assert d == D_MODEL and seq % TQA == 0 and t % TT_EXP == 0 and n_exp % TE_EXP == 0
    assert mix_norm_g.shape[0] == 1, "single-layer block"

    w = w_in[0]
    o_cq = 3 * A_WIDTH
    o_ckv = o_cq + Q_LORA
    o_kpe = o_ckv + KV_LORA
    o_g = o_kpe + B_ROPE
    wqkv = w[:, :o_cq].astype(BF16)
    wc = jnp.concatenate([w[:, o_cq:o_kpe], jnp.zeros((d, B_NOPE), F32), w[:, o_kpe:o_g],
                          jnp.zeros((d, LANES - B_NOPE - B_ROPE), F32)], axis=1).astype(BF16)
    wg = w[:, o_g:].astype(BF16)
    wqup = jnp.pad(w_q_up[0].reshape(Q_LORA, B_HEADS, B_NOPE + B_ROPE),
                   ((0, 0), (0, 0), (0, LANES - B_NOPE - B_ROPE))).reshape(Q_LORA, B_HEADS * LANES)
    wqup = wqup.astype(BF16)
    wkvup = w_kv_up[0].astype(BF16)
    row = lambda v: v.reshape(1, -1).astype(F32)
    tables = _rope_tables(positions)
    x2d = x.reshape(t, d)

    tm = TM_PROJ
    assert tm == TK, "one projection tile is one attention key block (transposed value tiles)"
    tok = lambda width: pl.BlockSpec((tm, width), lambda i: (i, 0))
    bf = lambda width: jax.ShapeDtypeStruct((t, width), BF16)
    tok_t = lambda width: pl.BlockSpec((None, width, tm), lambda i: (i, 0, 0))
    bf_t = lambda width: jax.ShapeDtypeStruct((t // tm, width, tm), BF16)
    qa, ka, vat, qb, kvb, kvbt, kpe, sg = pl.pallas_call(
        _inproj_kernel,
        grid=(t // tm,),
        in_specs=[tok(d), _full((1, d)), _full(wqkv.shape), _full(wc.shape), _full(wg.shape),
                  _full(wqup.shape), _full(wkvup.shape), _full((1, Q_LORA)), _full((1, KV_LORA))]
                 + [tok(LANES)] * 6,
        out_specs=[tok(A_WIDTH), tok(A_WIDTH), tok_t(A_WIDTH), tok(B_HEADS * LANES),
                   tok(B_HEADS * LANES), tok_t(B_HEADS * LANES), tok(LANES), tok(2 * d)],
        out_shape=[bf(A_WIDTH), bf(A_WIDTH), bf_t(A_WIDTH), bf(B_HEADS * LANES),
                   bf(B_HEADS * LANES), bf_t(B_HEADS * LANES), bf(LANES), bf(2 * d)],
        compiler_params=_params("parallel"),
        name="inproj",
    )(x2d, row(mix_norm_g[0]), wqkv, wc, wg, wqup, wkvup, row(q_norm_g[0]), row(kv_norm_g[0]),
      *tables)

    r3 = lambda a: a.reshape(bsz, seq, a.shape[-1])
    r4t = lambda a: a.reshape(bsz, seq // tm, a.shape[1], tm)
    nblk = seq // TK
    assert nblk <= LANES
    pairs = A_WIDTH // LANES
    ya = pl.pallas_call(
        _moba_kernel,
        grid=(bsz, pairs, seq // TQA),
        in_specs=[pl.BlockSpec((None, TQA, LANES), lambda b, g, i: (b, i, g)),
                  pl.BlockSpec((None, seq, LANES), lambda b, g, i: (b, 0, g)),
                  pl.BlockSpec((None, nblk, LANES, TK), lambda b, g, i: (b, 0, g, 0))],
        out_specs=pl.BlockSpec((None, TQA, LANES), lambda b, g, i: (b, i, g)),
        out_shape=jax.ShapeDtypeStruct((bsz, seq, A_WIDTH), BF16),
        scratch_shapes=[pltpu.VMEM((LANES, LANES), BF16), pltpu.VMEM((LANES, LANES), BF16),
                        pltpu.VMEM((2, LANES, TQA), F32)],
        compiler_params=_params("parallel", "parallel", "arbitrary"),
        name="moba",
    )(r3(qa), r3(ka), r4t(vat))

    yb = pl.pallas_call(
        _mla_kernel,
        grid=(bsz, B_HEADS // 2, seq // TQA),
        in_specs=[pl.BlockSpec((None, TQA, 2 * LANES), lambda b, g, i: (b, i, g)),
                  pl.BlockSpec((None, seq, 2 * LANES), lambda b, g, i: (b, 0, g)),
                  pl.BlockSpec((None, seq, LANES), lambda b, g, i: (b, 0, 0)),
                  pl.BlockSpec((None, nblk, 2 * LANES, TK), lambda b, g, i: (b, 0, g, 0))],
        out_specs=pl.BlockSpec((None, TQA, LANES), lambda b, g, i: (b, i, g)),
        out_shape=jax.ShapeDtypeStruct((bsz, seq, B_HEADS * B_V), BF16),
        scratch_shapes=[pltpu.VMEM((2, LANES, TQA), F32)],
        compiler_params=_params("parallel", "parallel", "arbitrary"),
        name="mla",
    )(r3(qb), r3(kvb), r3(kpe), r4t(kvbt))

    wba = w_branch_a[0].astype(BF16)
    wbb = w_branch_b[0].astype(BF16)
    wout = w_out[0].astype(BF16)
    wpq = w_peer_query[0].astype(BF16)
    sk1 = peer_sub_keys_1[0].astype(BF16)
    sk2 = peer_sub_keys_2[0].astype(BF16)
    score_spec = pl.BlockSpec((PEER_HEADS, PEER_NKEYS, tm), lambda i: (0, 0, i))
    score_shape = jax.ShapeDtypeStruct((PEER_HEADS, PEER_NKEYS, t), F32)
    x1, h2, s1t, s2t = pl.pallas_call(
        _merge_kernel,
        grid=(t // tm,),
        in_specs=[tok(d), tok(A_WIDTH), tok(B_HEADS * B_V), tok(2 * d), _full(wba.shape),
                  _full(wbb.shape), _full(wout.shape), _full((1, d)), _full(wpq.shape),
                  _full(sk1.shape), _full(sk2.shape)],
        out_specs=[tok(d), tok(d), score_spec, score_spec],
        out_shape=[jax.ShapeDtypeStruct((t, d), F32), bf(d), score_shape, score_shape],
        compiler_params=_params("parallel"),
        name="merge",
    )(x2d, ya.reshape(t, A_WIDTH), yb.reshape(t, B_HEADS * B_V), sg, wba, wbb, wout,
      row(ffn_norm_g[0]), wpq, sk1, sk2)

    tl = TL_ROUTE
    hs = pl.BlockSpec((PEER_HEADS, PEER_NKEYS, tl), lambda i: (0, 0, i))
    a_co, e2, tau = pl.pallas_call(
        _route_kernel,
        grid=(t // tl,),
        in_specs=[hs, hs],
        out_specs=[hs, hs, pl.BlockSpec((PEER_HEADS, 1, tl), lambda i: (0, 0, i))],
        out_shape=[score_shape, score_shape, jax.ShapeDtypeStruct((PEER_HEADS, 1, t), F32)],
        scratch_shapes=[pltpu.VMEM((PEER_TOPK, tl), F32)] * 3,
        compiler_params=_params("parallel"),
        name="route",
    )(s1t, s2t)

    u_bf = peer_expert_u[0].astype(BF16)
    vt_bf = peer_expert_v[0].T.astype(BF16)
    tt, te = TT_EXP, TE_EXP
    rt = pl.BlockSpec((PEER_HEADS, PEER_NKEYS, tt), lambda i, e: (0, 0, i))
    n_tiles = n_exp // te
    out = pl.pallas_call(
        _expert_kernel,
        grid=(t // tt, n_tiles + 1),
        in_specs=[pl.BlockSpec((tt, d), lambda i, e: (i, 0)),
                  pl.BlockSpec((te, d), lambda i, e: (jnp.minimum(e, n_tiles - 1), 0)),
                  pl.BlockSpec((d, te), lambda i, e: (0, jnp.maximum(e - 1, 0))),
                  rt, rt, rt, rt,
                  pl.BlockSpec((PEER_HEADS, 1, tt), lambda i, e: (0, 0, i)),
                  pl.BlockSpec((tt, d), lambda i, e: (i, 0)),
                  pl.BlockSpec((1, d), lambda i, e: (0, 0))],
        out_specs=pl.BlockSpec((tt, d), lambda i, e: (i, 0)),
        out_shape=jax.ShapeDtypeStruct((t, d), F32),
        scratch_shapes=[pltpu.VMEM((d, tt), F32), pltpu.VMEM((te, tt), F32),
                        pltpu.VMEM((te, tt), F32), pltpu.VMEM((te, tt), BF16)],
        compiler_params=_params("parallel", "arbitrary"),
        name="experts",
    )(h2, u_bf, vt_bf, s1t, s2t, a_co, e2, tau, x1, row(final_norm_g))
    return out.reshape(bsz, seq, d)
```

```python
import math

import jax
import jax.numpy as jnp
from jax import lax
from jax.experimental import pallas as pl
from jax.experimental.pallas import tpu as pltpu

F32 = jnp.float32
BF16 = jnp.bfloat16

D_MODEL = 1024
A_HEADS = 8
A_HEAD_DIM = 64
A_WIDTH = A_HEADS * A_HEAD_DIM
MOBA_BLOCK = 256
MOBA_TOPK = 3
B_HEADS = 8
B_NOPE = 64
B_ROPE = 32
B_V = 64
Q_LORA = 256
KV_LORA = 128
PEER_HEADS = 8
PEER_NKEYS = 128
PEER_HALF = 128
PEER_TOPK = 16
ROPE_THETA = 10000.0
RMS_EPS = 1e-6

LANES = 128
NEG_BIG = -1e30
LOG2E = math.log2(math.e)
VMEM_LIMIT = 56 * 1024 * 1024

TM_PROJ = 256
TK = MOBA_BLOCK
TQA = 8 * TK
TL_ROUTE = 256
TT_EXP = 512
TE_EXP = 512
TB_EXP = 256
TJ_EXP = 16


def _dot(a, b):
    return jnp.dot(a, b, preferred_element_type=F32)


def _dot_nt(a, b):
    return lax.dot_general(a, b, (((1,), (1,)), ((), ())), preferred_element_type=F32)


def _rms(x, g):
    return x * lax.rsqrt(jnp.mean(x * x, axis=-1, keepdims=True) + RMS_EPS) * g


def _inproj_kernel(x_ref, g_ref, wqkv_ref, wc_ref, wg_ref, wqup_ref, wkvup_ref, qng_ref,
                   kvng_ref, ca_ref, sa1_ref, sa2_ref, cb_ref, sb1_ref, sb2_ref,
                   qa_ref, ka_ref, vat_ref, qb_ref, kvb_ref, kvbt_ref, kpe_ref, sg_ref):
    h = _rms(x_ref[...], g_ref[...]).astype(BF16)

    qkv = _dot(h, wqkv_ref[...])
    ca, sa1, sa2 = ca_ref[...], sa1_ref[...], sa2_ref[...]

    def rope_a(xg):
        return xg * ca + pltpu.roll(xg, LANES - 32, 1) * sa1 + pltpu.roll(xg, 32, 1) * sa2

    for grp in range(A_WIDTH // LANES):
        lo = grp * LANES
        qa_ref[:, lo:lo + LANES] = (rope_a(qkv[:, lo:lo + LANES]) * (0.125 * LOG2E)).astype(BF16)
        ka_ref[:, lo:lo + LANES] = rope_a(qkv[:, A_WIDTH + lo:A_WIDTH + lo + LANES]).astype(BF16)
    vat_ref[...] = qkv[:, 2 * A_WIDTH:].T.astype(BF16)

    c = _dot(h, wc_ref[...])
    cb, sb1, sb2 = cb_ref[...], sb1_ref[...], sb2_ref[...]

    def rope_b(xg):
        return xg * cb + pltpu.roll(xg, LANES - 16, 1) * sb1 + pltpu.roll(xg, 16, 1) * sb2

    cq = _rms(c[:, :Q_LORA], qng_ref[...]).astype(BF16)
    qb = _dot(cq, wqup_ref[...]) * (LOG2E / math.sqrt(B_NOPE + B_ROPE))
    for hd in range(B_HEADS):
        lo = hd * LANES
        qb_ref[:, lo:lo + LANES] = rope_b(qb[:, lo:lo + LANES]).astype(BF16)
    ckv = _rms(c[:, Q_LORA:Q_LORA + KV_LORA], kvng_ref[...]).astype(BF16)
    kv = _dot(ckv, wkvup_ref[...])
    kvb_ref[...] = kv.astype(BF16)
    kvbt_ref[...] = kv.T.astype(BF16)
    kpe_ref[...] = rope_b(c[:, Q_LORA + KV_LORA:]).astype(BF16)

    sg_ref[...] = jax.nn.sigmoid(_dot(h, wg_ref[...])).astype(BF16)


def _attn_step(scores, carry, acc_ref, vts, c0=0):
    new, ps, alphas = [], [], []
    for s, (m, l) in zip(scores, carry):
        m_new = jnp.maximum(m[:, c0:], jnp.max(s, axis=0, keepdims=True))
        alpha = jnp.exp2(m[:, c0:] - m_new)
        p = jnp.exp2(s - m_new)
        l_new = alpha * l[:, c0:] + jnp.sum(p, axis=0, keepdims=True)
        if c0:
            m_new = jnp.concatenate([m[:, :c0], m_new], axis=1)
            l_new = jnp.concatenate([l[:, :c0], l_new], axis=1)
        new.append((m_new, l_new))
        ps.append(p.astype(BF16))
        alphas.append(alpha)
    for hh in range(2):
        acc_ref[hh, :, c0:] = alphas[hh] * acc_ref[hh, :, c0:] + _dot(vts[hh], ps[hh])
    return tuple(new)


def _attn_loop(i, step):
    per = TQA // TK
    init = tuple((jnp.full((1, TQA), NEG_BIG, F32), jnp.zeros((1, TQA), F32)) for _ in range(2))
    carry = lax.fori_loop(0, per * i, lambda j, c: step(j, c, None, 0), init)
    for a in range(per):
        c0 = a * TK
        krow = lax.broadcasted_iota(jnp.int32, (TK, TQA - c0), 0)
        qcol = lax.broadcasted_iota(jnp.int32, (TK, TQA - c0), 1)
        carry = step(per * i + a, carry, krow <= qcol, c0)
    return carry


def _transpose_bf16(x):
    return x.astype(F32).T.astype(BF16)


def _moba_kernel(q_ref, k_ref, vt_ref, o_ref, kmh_ref, kml_ref, acc_ref):
    i = pl.program_id(2)
    nblk = k_ref.shape[0] // TK

    @pl.when(i == 0)
    def _():
        rows = lax.broadcasted_iota(jnp.int32, (LANES, LANES), 0)
        km = jnp.zeros((LANES, LANES), F32)
        for j in range(nblk):
            blk = k_ref[j * TK:(j + 1) * TK, :].astype(F32)
            km = jnp.where(rows == j, jnp.mean(blk, axis=0, keepdims=True), km)
        hi = km.astype(BF16)
        kmh_ref[...] = hi
        kml_ref[...] = (km - hi.astype(F32)).astype(BF16)

    qt = _transpose_bf16(q_ref[...])
    sub = lax.broadcasted_iota(jnp.int32, (LANES, TQA), 0)
    col = lax.broadcasted_iota(jnp.int32, (LANES, TQA), 1)
    own = (TQA // TK) * i + col // TK
    zero = jnp.zeros_like(qt)
    qt_heads = (jnp.where(sub < A_HEAD_DIM, qt, zero), jnp.where(sub >= A_HEAD_DIM, qt, zero))

    q_aug = []
    for qh in qt_heads:
        gate = _dot(kmh_ref[...], qh) + _dot(kml_ref[...], qh)
        valid = sub < own
        g = jnp.where(valid, gate, -jnp.inf)[:nblk]
        thr = jnp.max(g, axis=0, keepdims=True)
        for _ in range(MOBA_TOPK - 1):
            g = jnp.where(g >= thr, -jnp.inf, g)
            thr = jnp.max(g, axis=0, keepdims=True)
        keep = (valid & (gate >= thr)) | (sub == own)
        bias = jnp.where(keep, 0.0, NEG_BIG).astype(BF16)
        q_aug.append(jnp.concatenate([qh, bias], axis=0))

    lane = lax.broadcasted_iota(jnp.int32, (TK, LANES), 1)
    acc_ref[...] = jnp.zeros_like(acc_ref)

    def step(j, carry, causal, c0):
        start = pl.multiple_of(j * TK, TK)
        onehot = jnp.where(lane == j, 1.0, 0.0).astype(BF16)
        ka = jnp.concatenate([k_ref[pl.ds(start, TK), :], onehot], axis=1)
        scores = [_dot(ka, qa[:, c0:]) for qa in q_aug]
        if causal is not None:
            scores = [jnp.where(causal, s, NEG_BIG) for s in scores]
        vt = vt_ref[j]
        return _attn_step(scores, carry, acc_ref, (vt, vt), c0)

    carry = _attn_loop(i, step)
    outs = [acc_ref[hh] / carry[hh][1] for hh in range(2)]
    o_ref[...] = jnp.where(sub < A_HEAD_DIM, outs[0], outs[1]).T.astype(o_ref.dtype)


def _mla_kernel(q_ref, kv_ref, kpe_ref, kvt_ref, o_ref, acc_ref):
    i = pl.program_id(2)
    lane = lax.broadcasted_iota(jnp.int32, (TK, LANES), 1)
    qts = (_transpose_bf16(q_ref[:, :LANES]), _transpose_bf16(q_ref[:, LANES:]))
    acc_ref[...] = jnp.zeros_like(acc_ref)

    def step(j, carry, causal, c0):
        start = pl.multiple_of(j * TK, TK)
        kpe = kpe_ref[pl.ds(start, TK), :]
        scores = []
        for hh in range(2):
            kv = kv_ref[pl.ds(start, TK), hh * LANES:(hh + 1) * LANES]
            key = jnp.where(lane < B_NOPE, kv, kpe)
            scores.append(_dot(key, qts[hh][:, c0:]))
        if causal is not None:
            scores = [jnp.where(causal, s, NEG_BIG) for s in scores]
        return _attn_step(scores, carry, acc_ref,
                          (kvt_ref[j, :LANES, :], kvt_ref[j, LANES:, :]), c0)

    carry = _attn_loop(i, step)
    outs = [(acc_ref[hh] / carry[hh][1])[B_NOPE:, :] for hh in range(2)]
    o_ref[...] = jnp.concatenate(outs, axis=0).T.astype(o_ref.dtype)


def _merge_kernel(x_ref, ya_ref, yb_ref, sg_ref, wba_ref, wbb_ref, wout_ref, fg_ref, wpq_ref,
                  sk1_ref, sk2_ref, x1_ref, h2_ref, s1_ref, s2_ref):
    merged = (sg_ref[:, :D_MODEL].astype(F32) * _dot(ya_ref[...], wba_ref[...])
              + sg_ref[:, D_MODEL:].astype(F32) * _dot(yb_ref[...], wbb_ref[...]))
    x1 = x_ref[...] + _dot(merged.astype(BF16), wout_ref[...])
    x1_ref[...] = x1
    h2 = _rms(x1, fg_ref[...]).astype(BF16)
    h2_ref[...] = h2
    qp = _dot(h2, wpq_ref[...]).astype(BF16)
    for hd in range(PEER_HEADS):
        lo = 2 * hd * PEER_HALF
        s1_ref[hd] = _dot_nt(sk1_ref[hd], qp[:, lo:lo + PEER_HALF])
        s2_ref[hd] = _dot_nt(sk2_ref[hd], qp[:, lo + PEER_HALF:lo + 2 * PEER_HALF])


def _route_kernel(s1_ref, s2_ref, a_ref, e2_ref, tau_ref, v1_ref, v2_ref, best_ref):
    def top16(src, dst_ref):
        cur = src
        for r in range(PEER_TOPK):
            mx = jnp.max(cur, axis=0, keepdims=True)
            dst_ref[r:r + 1, :] = mx
            cur = jnp.where(cur == mx, -jnp.inf, cur)

    def head(hd, carry):
        s1 = s1_ref[hd]
        s2 = s2_ref[hd]
        top16(s1, v1_ref)
        top16(s2, v2_ref)
        v1 = v1_ref[...]
        v2 = v2_ref[...]
        half = PEER_TOPK // 2
        cand = jnp.concatenate(
            [v1 + v2[0:1]] + [v1[:half] + v2[r:r + 1] for r in range(1, half)]
            + [v1[0:1] + v2[half:]], axis=0)
        top16(cand, best_ref)
        best = best_ref[...]
        z = jnp.sum(jnp.exp(best - best[0:1, :]), axis=0, keepdims=True)
        tau_ref[hd] = best[PEER_TOPK - 1:PEER_TOPK, :]
        a_ref[hd] = jnp.exp(s1 - v1_ref[0:1, :]) / z
        e2_ref[hd] = jnp.exp(s2 - v2_ref[0:1, :])
        return carry

    lax.fori_loop(0, PEER_HEADS, head, 0)


def _expert_kernel(h2_ref, u_ref, vt_ref, s1_ref, s2_ref, a_ref, e2_ref, tau_ref, x1_ref, fg_ref,
                   o_ref, acc_ref, act0_ref, act1_ref, wa_ref):
    e = pl.program_id(1)
    n_tiles = pl.num_programs(1) - 1
    slabs = TE_EXP // PEER_NKEYS
    prev = jnp.maximum(e - 1, 0)

    @pl.when(e == 0)
    def _():
        acc_ref[...] = jnp.zeros_like(acc_ref)
        act1_ref[...] = jnp.zeros_like(act1_ref)

    def body(act_w_ref, act_r_ref):
        for tb in range(TT_EXP // TB_EXP):
            ts = slice(tb * TB_EXP, (tb + 1) * TB_EXP)
            pre = _dot_nt(u_ref[...], h2_ref[ts, :])
            act_w_ref[:, ts] = 0.5 * pre * (1.0 + lax.erf(pre * math.sqrt(0.5)))
            tau = [tau_ref[hd, :, ts] for hd in range(PEER_HEADS)]
            for sl in range(slabs):
                i1 = prev * slabs + sl
                s1r = [s1_ref[hd, pl.ds(i1, 1), ts] for hd in range(PEER_HEADS)]
                ar = [a_ref[hd, pl.ds(i1, 1), ts] for hd in range(PEER_HEADS)]
                for jt in range(PEER_NKEYS // TJ_EXP):
                    rj = slice(jt * TJ_EXP, (jt + 1) * TJ_EXP)
                    w = None
                    for hd in range(PEER_HEADS):
                        cand = s2_ref[hd, rj, ts] + s1r[hd]
                        term = jnp.where(cand >= tau[hd], e2_ref[hd, rj, ts], 0.0) * ar[hd]
                        w = term if w is None else w + term
                    ro = slice(sl * PEER_NKEYS + jt * TJ_EXP, sl * PEER_NKEYS + (jt + 1) * TJ_EXP)
                    wa_ref[ro, ts] = (w * act_r_ref[ro, ts]).astype(BF16)
            acc_ref[:, ts] += _dot(vt_ref[...], wa_ref[:, ts])

    @pl.when(e % 2 == 0)
    def _():
        body(act0_ref, act1_ref)

    @pl.when(e % 2 == 1)
    def _():
        body(act1_ref, act0_ref)

    @pl.when(e == n_tiles)
    def _():
        o_ref[...] = _rms(x1_ref[...] + acc_ref[...].T, fg_ref[...])


def _params(*sem):
    return pltpu.CompilerParams(dimension_semantics=sem, vmem_limit_bytes=VMEM_LIMIT)


def _full(shape):
    nd = len(shape)
    return pl.BlockSpec(shape, lambda *_: (0,) * nd)


def _rope_tables(positions):
    pos = positions.reshape(-1).astype(F32)[:, None]

    def cs(dim):
        inv_freq = ROPE_THETA ** (-jnp.arange(0, dim, 2, dtype=F32) / dim)
        ang = pos * inv_freq
        return jnp.cos(ang), jnp.sin(ang)

    ca, sa = cs(A_HEAD_DIM)
    za = jnp.zeros_like(sa)
    cb, sb = cs(B_ROPE)
    t = pos.shape[0]
    zb = lambda n: jnp.zeros((t, n), F32)
    ob = lambda n: jnp.ones((t, n), F32)
    return (jnp.concatenate([ca] * 4, axis=1),
            jnp.concatenate([-sa, za, -sa, za], axis=1),
            jnp.concatenate([za, sa, za, sa], axis=1),
            jnp.concatenate([ob(B_NOPE), cb, cb, ob(32)], axis=1),
            jnp.concatenate([zb(B_NOPE), -sb, zb(16), zb(32)], axis=1),
            jnp.concatenate([zb(B_NOPE), zb(16), sb, zb(32)], axis=1))


def kernel(x, positions, mix_norm_g, w_in, q_norm_g, w_q_up, kv_norm_g, w_kv_up, w_branch_a,
           w_branch_b, w_out, ffn_norm_g, w_peer_query, peer_sub_keys_1, peer_sub_keys_2,
           peer_expert_u, peer_expert_v, final_norm_g):
    bsz, seq, d = x.shape
    t = bsz * seq
    n_exp = peer_expert_u.shape[1]
    assert d == D_MODEL and seq % TQA == 0 and t % TT_EXP == 0 and n_exp % TE_EXP == 0
    assert mix_norm_g.shape[0] == 1, "single-layer block"

    w = w_in[0]
    o_cq = 3 * A_WIDTH
    o_ckv = o_cq + Q_LORA
    o_kpe = o_ckv + KV_LORA
    o_g = o_kpe + B_ROPE
    wqkv = w[:, :o_cq].astype(BF16)
    wc = jnp.concatenate([w[:, o_cq:o_kpe], jnp.zeros((d, B_NOPE), F32), w[:, o_kpe:o_g],
                          jnp.zeros((d, LANES - B_NOPE - B_ROPE), F32)], axis=1).astype(BF16)
    wg = w[:, o_g:].astype(BF16)
    wqup = jnp.pad(w_q_up[0].reshape(Q_LORA, B_HEADS, B_NOPE + B_ROPE),
                   ((0, 0), (0, 0), (0, LANES - B_NOPE - B_ROPE))).reshape(Q_LORA, B_HEADS * LANES)
    wqup = wqup.astype(BF16)
    wkvup = w_kv_up[0].astype(BF16)
    row = lambda v: v.reshape(1, -1).astype(F32)
    tables = _rope_tables(positions)
    x2d = x.reshape(t, d)

    tm = TM_PROJ
    assert tm == TK, "one projection tile is one attention key block (transposed value tiles)"
    tok = lambda width: pl.BlockSpec((tm, width), lambda i: (i, 0))
    bf = lambda width: jax.ShapeDtypeStruct((t, width), BF16)
    tok_t = lambda width: pl.BlockSpec((None, width, tm), lambda i: (i, 0, 0))
    bf_t = lambda width: jax.ShapeDtypeStruct((t // tm, width, tm), BF16)
    qa, ka, vat, qb, kvb, kvbt, kpe, sg = pl.pallas_call(
        _inproj_kernel,
        grid=(t // tm,),
        in_specs=[tok(d), _full((1, d)), _full(wqkv.shape), _full(wc.shape), _full(wg.shape),
                  _full(wqup.shape), _full(wkvup.shape), _full((1, Q_LORA)), _full((1, KV_LORA))]
                 + [tok(LANES)] * 6,
        out_specs=[tok(A_WIDTH), tok(A_WIDTH), tok_t(A_WIDTH), tok(B_HEADS * LANES),
                   tok(B_HEADS * LANES), tok_t(B_HEADS * LANES), tok(LANES), tok(2 * d)],
        out_shape=[bf(A_WIDTH), bf(A_WIDTH), bf_t(A_WIDTH), bf(B_HEADS * LANES),
                   bf(B_HEADS * LANES), bf_t(B_HEADS * LANES), bf(LANES), bf(2 * d)],
        compiler_params=_params("parallel"),
        name="inproj",
    )(x2d, row(mix_norm_g[0]), wqkv, wc, wg, wqup, wkvup, row(q_norm_g[0]), row(kv_norm_g[0]),
      *tables)

    r3 = lambda a: a.reshape(bsz, seq, a.shape[-1])
    r4t = lambda a: a.reshape(bsz, seq // tm, a.shape[1], tm)
    nblk = seq // TK
    assert nblk <= LANES
    pairs = A_WIDTH // LANES
    ya = pl.pallas_call(
        _moba_kernel,
        grid=(bsz, pairs, seq // TQA),
        in_specs=[pl.BlockSpec((None, TQA, LANES), lambda b, g, i: (b, i, g)),
                  pl.BlockSpec((None, seq, LANES), lambda b, g, i: (b, 0, g)),
                  pl.BlockSpec((None, nblk, LANES, TK), lambda b, g, i: (b, 0, g, 0))],
        out_specs=pl.BlockSpec((None, TQA, LANES), lambda b, g, i: (b, i, g)),
        out_shape=jax.ShapeDtypeStruct((bsz, seq, A_WIDTH), BF16),
        scratch_shapes=[pltpu.VMEM((LANES, LANES), BF16), pltpu.VMEM((LANES, LANES), BF16),
                        pltpu.VMEM((2, LANES, TQA), F32)],
        compiler_params=_params("parallel", "parallel", "arbitrary"),
        name="moba",
    )(r3(qa), r3(ka), r4t(vat))

    yb = pl.pallas_call(
        _mla_kernel,
        grid=(bsz, B_HEADS // 2, seq // TQA),
        in_specs=[pl.BlockSpec((None, TQA, 2 * LANES), lambda b, g, i: (b, i, g)),
                  pl.BlockSpec((None, seq, 2 * LANES), lambda b, g, i: (b, 0, g)),
                  pl.BlockSpec((None, seq, LANES), lambda b, g, i: (b, 0, 0)),
                  pl.BlockSpec((None, nblk, 2 * LANES, TK), lambda b, g, i: (b, 0, g, 0))],
        out_specs=pl.BlockSpec((None, TQA, LANES), lambda b, g, i: (b, i, g)),
        out_shape=jax.ShapeDtypeStruct((bsz, seq, B_HEADS * B_V), BF16),
        scratch_shapes=[pltpu.VMEM((2, LANES, TQA), F32)],
        compiler_params=_params("parallel", "parallel", "arbitrary"),
        name="mla",
    )(r3(qb), r3(kvb), r3(kpe), r4t(kvbt))

    wba = w_branch_a[0].astype(BF16)
    wbb = w_branch_b[0].astype(BF16)
    wout = w_out[0].astype(BF16)
    wpq = w_peer_query[0].astype(BF16)
    sk1 = peer_sub_keys_1[0].astype(BF16)
    sk2 = peer_sub_keys_2[0].astype(BF16)
    score_spec = pl.BlockSpec((PEER_HEADS, PEER_NKEYS, tm), lambda i: (0, 0, i))
    score_shape = jax.ShapeDtypeStruct((PEER_HEADS, PEER_NKEYS, t), F32)
    x1, h2, s1t, s2t = pl.pallas_call(
        _merge_kernel,
        grid=(t // tm,),
        in_specs=[tok(d), tok(A_WIDTH), tok(B_HEADS * B_V), tok(2 * d), _full(wba.shape),
                  _full(wbb.shape), _full(wout.shape), _full((1, d)), _full(wpq.shape),
                  _full(sk1.shape), _full(sk2.shape)],
        out_specs=[tok(d), tok(d), score_spec, score_spec],
        out_shape=[jax.ShapeDtypeStruct((t, d), F32), bf(d), score_shape, score_shape],
        compiler_params=_params("parallel"),
        name="merge",
    )(x2d, ya.reshape(t, A_WIDTH), yb.reshape(t, B_HEADS * B_V), sg, wba, wbb, wout,
      row(ffn_norm_g[0]), wpq, sk1, sk2)

    tl = TL_ROUTE
    hs = pl.BlockSpec((PEER_HEADS, PEER_NKEYS, tl), lambda i: (0, 0, i))
    a_co, e2, tau = pl.pallas_call(
        _route_kernel,
        grid=(t // tl,),
        in_specs=[hs, hs],
        out_specs=[hs, hs, pl.BlockSpec((PEER_HEADS, 1, tl), lambda i: (0, 0, i))],
        out_shape=[score_shape, score_shape, jax.ShapeDtypeStruct((PEER_HEADS, 1, t), F32)],
        scratch_shapes=[pltpu.VMEM((PEER_TOPK, tl), F32)] * 3,
        compiler_params=_params("parallel"),
        name="route",
    )(s1t, s2t)

    u_bf = peer_expert_u[0].astype(BF16)
    vt_bf = peer_expert_v[0].T.astype(BF16)
    tt, te = TT_EXP, TE_EXP
    rt = pl.BlockSpec((PEER_HEADS, PEER_NKEYS, tt), lambda i, e: (0, 0, i))
    n_tiles = n_exp // te
    out = pl.pallas_call(
        _expert_kernel,
        grid=(t // tt, n_tiles + 1),
        in_specs=[pl.BlockSpec((tt, d), lambda i, e: (i, 0)),
                  pl.BlockSpec((te, d), lambda i, e: (jnp.minimum(e, n_tiles - 1), 0)),
                  pl.BlockSpec((d, te), lambda i, e: (0, jnp.maximum(e - 1, 0))),
                  rt, rt, rt, rt,
                  pl.BlockSpec((PEER_HEADS, 1, tt), lambda i, e: (0, 0, i)),
                  pl.BlockSpec((tt, d), lambda i, e: (i, 0)),
                  pl.BlockSpec((1, d), lambda i, e: (0, 0))],
        out_specs=pl.BlockSpec((tt, d), lambda i, e: (i, 0)),
        out_shape=jax.ShapeDtypeStruct((t, d), F32),
        scratch_shapes=[pltpu.VMEM((d, tt), F32), pltpu.VMEM((te, tt), F32),
                        pltpu.VMEM((te, tt), F32), pltpu.VMEM((te, tt), BF16)],
        compiler_params=_params("parallel", "arbitrary"),
        name="experts",
    )(h2, u_bf, vt_bf, s1t, s2t, a_co, e2, tau, x1, row(final_norm_g))
    return out.reshape(bsz, seq, d)
```

```python
import math

import jax
import jax.numpy as jnp
from jax import lax
from jax.experimental import pallas as pl
from jax.experimental.pallas import tpu as pltpu

F32 = jnp.float32
BF16 = jnp.bfloat16

D_MODEL = 1024
A_HEADS = 8
A_HEAD_DIM = 64
A_WIDTH = A_HEADS * A_HEAD_DIM
MOBA_BLOCK = 256
MOBA_TOPK = 3
B_HEADS = 8
B_NOPE = 64
B_ROPE = 32
B_V = 64
Q_LORA = 256
KV_LORA = 128
PEER_HEADS = 8
PEER_NKEYS = 128
PEER_HALF = 128
PEER_TOPK = 16
ROPE_THETA = 10000.0
RMS_EPS = 1e-6

LANES = 128
NEG_BIG = -1e30
LOG2E = math.log2(math.e)
VMEM_LIMIT = 56 * 1024 * 1024

TM_PROJ = 256
TK = MOBA_BLOCK
TQA = 8 * TK
TL_ROUTE = 256
TT_EXP = 512
TE_EXP = 512
TB_EXP = 256
TJ_EXP = 32


def _dot(a, b):
    return jnp.dot(a, b, preferred_element_type=F32)


def _dot_nt(a, b):
    return lax.dot_general(a, b, (((1,), (1,)), ((), ())), preferred_element_type=F32)


def _rms(x, g):
    return x * lax.rsqrt(jnp.mean(x * x, axis=-1, keepdims=True) + RMS_EPS) * g


def _inproj_kernel(x_ref, g_ref, wqkv_ref, wc_ref, wg_ref, wqup_ref, wkvup_ref, qng_ref,
                   kvng_ref, ca_ref, sa1_ref, sa2_ref, cb_ref, sb1_ref, sb2_ref,
                   qa_ref, ka_ref, vat_ref, qb_ref, kvb_ref, kvbt_ref, kpe_ref, sg_ref):
    h = _rms(x_ref[...], g_ref[...]).astype(BF16)

    qkv = _dot(h, wqkv_ref[...])
    ca, sa1, sa2 = ca_ref[...], sa1_ref[...], sa2_ref[...]

    def rope_a(xg):
        return xg * ca + pltpu.roll(xg, LANES - 32, 1) * sa1 + pltpu.roll(xg, 32, 1) * sa2

    for grp in range(A_WIDTH // LANES):
        lo = grp * LANES
        qa_ref[:, lo:lo + LANES] = (rope_a(qkv[:, lo:lo + LANES]) * (0.125 * LOG2E)).astype(BF16)
        ka_ref[:, lo:lo + LANES] = rope_a(qkv[:, A_WIDTH + lo:A_WIDTH + lo + LANES]).astype(BF16)
    vat_ref[...] = qkv[:, 2 * A_WIDTH:].T.astype(BF16)

    c = _dot(h, wc_ref[...])
    cb, sb1, sb2 = cb_ref[...], sb1_ref[...], sb2_ref[...]

    def rope_b(xg):
        return xg * cb + pltpu.roll(xg, LANES - 16, 1) * sb1 + pltpu.roll(xg, 16, 1) * sb2

    cq = _rms(c[:, :Q_LORA], qng_ref[...]).astype(BF16)
    qb = _dot(cq, wqup_ref[...]) * (LOG2E / math.sqrt(B_NOPE + B_ROPE))
    for hd in range(B_HEADS):
        lo = hd * LANES
        qb_ref[:, lo:lo + LANES] = rope_b(qb[:, lo:lo + LANES]).astype(BF16)
    ckv = _rms(c[:, Q_LORA:Q_LORA + KV_LORA], kvng_ref[...]).astype(BF16)
    kv = _dot(ckv, wkvup_ref[...])
    kvb_ref[...] = kv.astype(BF16)
    kvbt_ref[...] = kv.T.astype(BF16)
    kpe_ref[...] = rope_b(c[:, Q_LORA + KV_LORA:]).astype(BF16)

    sg_ref[...] = jax.nn.sigmoid(_dot(h, wg_ref[...])).astype(BF16)


def _attn_step(scores, carry, acc_ref, vts, c0=0):
    new, ps, alphas = [], [], []
    for s, (m, l) in zip(scores, carry):
        m_new = jnp.maximum(m[:, c0:], jnp.max(s, axis=0, keepdims=True))
        alpha = jnp.exp2(m[:, c0:] - m_new)
        p = jnp.exp2(s - m_new)
        l_new = alpha * l[:, c0:] + jnp.sum(p, axis=0, keepdims=True)
        if c0:
            m_new = jnp.concatenate([m[:, :c0], m_new], axis=1)
            l_new = jnp.concatenate([l[:, :c0], l_new], axis=1)
        new.append((m_new, l_new))
        ps.append(p.astype(BF16))
        alphas.append(alpha)
    for hh in range(2):
        acc_ref[hh, :, c0:] = alphas[hh] * acc_ref[hh, :, c0:] + _dot(vts[hh], ps[hh])
    return tuple(new)


def _attn_loop(i, step):
    per = TQA // TK
    init = tuple((jnp.full((1, TQA), NEG_BIG, F32), jnp.zeros((1, TQA), F32)) for _ in range(2))
    carry = lax.fori_loop(0, per * i, lambda j, c: step(j, c, None, 0), init)
    for a in range(per):
        c0 = a * TK
        krow = lax.broadcasted_iota(jnp.int32, (TK, TQA - c0), 0)
        qcol = lax.broadcasted_iota(jnp.int32, (TK, TQA - c0), 1)
        carry = step(per * i + a, carry, krow <= qcol, c0)
    return carry


def _transpose_bf16(x):
    return x.astype(F32).T.astype(BF16)


def _moba_kernel(q_ref, k_ref, vt_ref, o_ref, kmh_ref, kml_ref, acc_ref):
    i = pl.program_id(2)
    nblk = k_ref.shape[0] // TK

    @pl.when(i == 0)
    def _():
        rows = lax.broadcasted_iota(jnp.int32, (LANES, LANES), 0)
        km = jnp.zeros((LANES, LANES), F32)
        for j in range(nblk):
            blk = k_ref[j * TK:(j + 1) * TK, :].astype(F32)
            km = jnp.where(rows == j, jnp.mean(blk, axis=0, keepdims=True), km)
        hi = km.astype(BF16)
        kmh_ref[...] = hi
        kml_ref[...] = (km - hi.astype(F32)).astype(BF16)

    qt = _transpose_bf16(q_ref[...])
    sub = lax.broadcasted_iota(jnp.int32, (LANES, TQA), 0)
    col = lax.broadcasted_iota(jnp.int32, (LANES, TQA), 1)
    own = (TQA // TK) * i + col // TK
    zero = jnp.zeros_like(qt)
    qt_heads = (jnp.where(sub < A_HEAD_DIM, qt, zero), jnp.where(sub >= A_HEAD_DIM, qt, zero))

    q_aug = []
    for qh in qt_heads:
        gate = _dot(kmh_ref[...], qh) + _dot(kml_ref[...], qh)
        valid = sub < own
        g = jnp.where(valid, gate, -jnp.inf)[:nblk]
        thr = jnp.max(g, axis=0, keepdims=True)
        for _ in range(MOBA_TOPK - 1):
            g = jnp.where(g >= thr, -jnp.inf, g)
            thr = jnp.max(g, axis=0, keepdims=True)
        keep = (valid & (gate >= thr)) | (sub == own)
        bias = jnp.where(keep, 0.0, NEG_BIG).astype(BF16)
        q_aug.append(jnp.concatenate([qh, bias], axis=0))

    lane = lax.broadcasted_iota(jnp.int32, (TK, LANES), 1)
    acc_ref[...] = jnp.zeros_like(acc_ref)

    def step(j, carry, causal, c0):
        start = pl.multiple_of(j * TK, TK)
        onehot = jnp.where(lane == j, 1.0, 0.0).astype(BF16)
        ka = jnp.concatenate([k_ref[pl.ds(start, TK), :], onehot], axis=1)
        scores = [_dot(ka, qa[:, c0:]) for qa in q_aug]
        if causal is not None:
            scores = [jnp.where(causal, s, NEG_BIG) for s in scores]
        vt = vt_ref[j]
        return _attn_step(scores, carry, acc_ref, (vt, vt), c0)

    carry = _attn_loop(i, step)
    outs = [acc_ref[hh] / carry[hh][1] for hh in range(2)]
    o_ref[...] = jnp.where(sub < A_HEAD_DIM, outs[0], outs[1]).T.astype(o_ref.dtype)


def _mla_kernel(q_ref, kv_ref, kpe_ref, kvt_ref, o_ref, acc_ref):
    i = pl.program_id(2)
    lane = lax.broadcasted_iota(jnp.int32, (TK, LANES), 1)
    qts = (_transpose_bf16(q_ref[:, :LANES]), _transpose_bf16(q_ref[:, LANES:]))
    acc_ref[...] = jnp.zeros_like(acc_ref)

    def step(j, carry, causal, c0):
        start = pl.multiple_of(j * TK, TK)
        kpe = kpe_ref[pl.ds(start, TK), :]
        scores = []
        for hh in range(2):
            kv = kv_ref[pl.ds(start, TK), hh * LANES:(hh + 1) * LANES]
            key = jnp.where(lane < B_NOPE, kv, kpe)
            scores.append(_dot(key, qts[hh][:, c0:]))
        if causal is not None:
            scores = [jnp.where(causal, s, NEG_BIG) for s in scores]
        return _attn_step(scores, carry, acc_ref,
                          (kvt_ref[j, :LANES, :], kvt_ref[j, LANES:, :]), c0)

    carry = _attn_loop(i, step)
    outs = [(acc_ref[hh] / carry[hh][1])[B_NOPE:, :] for hh in range(2)]
    o_ref[...] = jnp.concatenate(outs, axis=0).T.astype(o_ref.dtype)


def _merge_kernel(x_ref, ya_ref, yb_ref, sg_ref, wba_ref, wbb_ref, wout_ref, fg_ref, wpq_ref,
                  sk1_ref, sk2_ref, x1_ref, h2_ref, s1_ref, s2_ref):
    merged = (sg_ref[:, :D_MODEL].astype(F32) * _dot(ya_ref[...], wba_ref[...])
              + sg_ref[:, D_MODEL:].astype(F32) * _dot(yb_ref[...], wbb_ref[...]))
    x1 = x_ref[...] + _dot(merged.astype(BF16), wout_ref[...])
    x1_ref[...] = x1
    h2 = _rms(x1, fg_ref[...]).astype(BF16)
    h2_ref[...] = h2
    qp = _dot(h2, wpq_ref[...]).astype(BF16)
    for hd in range(PEER_HEADS):
        lo = 2 * hd * PEER_HALF
        s1_ref[hd] = _dot_nt(sk1_ref[hd], qp[:, lo:lo + PEER_HALF])
        s2_ref[hd] = _dot_nt(sk2_ref[hd], qp[:, lo + PEER_HALF:lo + 2 * PEER_HALF])


def _route_kernel(s1_ref, s2_ref, a_ref, e2_ref, tau_ref, v1_ref, v2_ref, best_ref):
    def top16(src, dst_ref):
        cur = src
        for r in range(PEER_TOPK):
            mx = jnp.max(cur, axis=0, keepdims=True)
            dst_ref[r:r + 1, :] = mx
            cur = jnp.where(cur == mx, -jnp.inf, cur)

    def head(hd, carry):
        s1 = s1_ref[hd]
        s2 = s2_ref[hd]
        top16(s1, v1_ref)
        top16(s2, v2_ref)
        v1 = v1_ref[...]
        v2 = v2_ref[...]
        half = PEER_TOPK // 2
        cand = jnp.concatenate(
            [v1 + v2[0:1]] + [v1[:half] + v2[r:r + 1] for r in range(1, half)]
            + [v1[0:1] + v2[half:]], axis=0)
        top16(cand, best_ref)
        best = best_ref[...]
        z = jnp.sum(jnp.exp(best - best[0:1, :]), axis=0, keepdims=True)
        tau_ref[hd] = best[PEER_TOPK - 1:PEER_TOPK, :]
        a_ref[hd] = jnp.exp(s1 - v1_ref[0:1, :]) / z
        e2_ref[hd] = jnp.exp(s2 - v2_ref[0:1, :])
        return carry

    lax.fori_loop(0, PEER_HEADS, head, 0)


def _expert_kernel(h2_ref, u_ref, vt_ref, s1_ref, s2_ref, a_ref, e2_ref, tau_ref, x1_ref, fg_ref,
                   o_ref, acc_ref, act0_ref, act1_ref, wa_ref):
    e = pl.program_id(1)
    n_tiles = pl.num_programs(1) - 1
    slabs = TE_EXP // PEER_NKEYS
    prev = jnp.maximum(e - 1, 0)

    @pl.when(e == 0)
    def _():
        acc_ref[...] = jnp.zeros_like(acc_ref)
        act1_ref[...] = jnp.zeros_like(act1_ref)

    def body(act_w_ref, act_r_ref):
        for tb in range(TT_EXP // TB_EXP):
            ts = slice(tb * TB_EXP, (tb + 1) * TB_EXP)
            pre = _dot_nt(u_ref[...], h2_ref[ts, :])
            act_w_ref[:, ts] = 0.5 * pre * (1.0 + lax.erf(pre * math.sqrt(0.5)))
            tau = [tau_ref[hd, :, ts] for hd in range(PEER_HEADS)]
            for sl in range(slabs):
                i1 = prev * slabs + sl
                s1r = [s1_ref[hd, pl.ds(i1, 1), ts] for hd in range(PEER_HEADS)]
                ar = [a_ref[hd, pl.ds(i1, 1), ts] for hd in range(PEER_HEADS)]
                for jt in range(PEER_NKEYS // TJ_EXP):
                    rj = slice(jt * TJ_EXP, (jt + 1) * TJ_EXP)
                    w = None
                    for hd in range(PEER_HEADS):
                        cand = s2_ref[hd, rj, ts] + s1r[hd]
                        term = jnp.where(cand >= tau[hd], e2_ref[hd, rj, ts], 0.0) * ar[hd]
                        w = term if w is None else w + term
                    ro = slice(sl * PEER_NKEYS + jt * TJ_EXP, sl * PEER_NKEYS + (jt + 1) * TJ_EXP)
                    wa_ref[ro, ts] = (w * act_r_ref[ro, ts]).astype(BF16)
            acc_ref[:, ts] += _dot(vt_ref[...], wa_ref[:, ts])

    @pl.when(e % 2 == 0)
    def _():
        body(act0_ref, act1_ref)

    @pl.when(e % 2 == 1)
    def _():
        body(act1_ref, act0_ref)

    @pl.when(e == n_tiles)
    def _():
        o_ref[...] = _rms(x1_ref[...] + acc_ref[...].T, fg_ref[...])


def _params(*sem):
    return pltpu.CompilerParams(dimension_semantics=sem, vmem_limit_bytes=VMEM_LIMIT)


def _full(shape):
    nd = len(shape)
    return pl.BlockSpec(shape, lambda *_: (0,) * nd)


def _rope_tables(positions):
    pos = positions.reshape(-1).astype(F32)[:, None]

    def cs(dim):
        inv_freq = ROPE_THETA ** (-jnp.arange(0, dim, 2, dtype=F32) / dim)
        ang = pos * inv_freq
        return jnp.cos(ang), jnp.sin(ang)

    ca, sa = cs(A_HEAD_DIM)
    za = jnp.zeros_like(sa)
    cb, sb = cs(B_ROPE)
    t = pos.shape[0]
    zb = lambda n: jnp.zeros((t, n), F32)
    ob = lambda n: jnp.ones((t, n), F32)
    return (jnp.concatenate([ca] * 4, axis=1),
            jnp.concatenate([-sa, za, -sa, za], axis=1),
            jnp.concatenate([za, sa, za, sa], axis=1),
            jnp.concatenate([ob(B_NOPE), cb, cb, ob(32)], axis=1),
            jnp.concatenate([zb(B_NOPE), -sb, zb(16), zb(32)], axis=1),
            jnp.concatenate([zb(B_NOPE), zb(16), sb, zb(32)], axis=1))


def kernel(x, positions, mix_norm_g, w_in, q_norm_g, w_q_up, kv_norm_g, w_kv_up, w_branch_a,
           w_branch_b, w_out, ffn_norm_g, w_peer_query, peer_sub_keys_1, peer_sub_keys_2,
           peer_expert_u, peer_expert_v, final_norm_g):
    bsz, seq, d = x.shape
    t = bsz * seq
    n_exp = peer_expert_u.shape[1]
    assert d == D_MODEL and seq % TQA == 0 and t % TT_EXP == 0 and n_exp % TE_EXP == 0
    assert mix_norm_g.shape[0] == 1, "single-layer block"

    w = w_in[0]
    o_cq = 3 * A_WIDTH
    o_ckv = o_cq + Q_LORA
    o_kpe = o_ckv + KV_LORA
    o_g = o_kpe + B_ROPE
    wqkv = w[:, :o_cq].astype(BF16)
    wc = jnp.concatenate([w[:, o_cq:o_kpe], jnp.zeros((d, B_NOPE), F32), w[:, o_kpe:o_g],
                          jnp.zeros((d, LANES - B_NOPE - B_ROPE), F32)], axis=1).astype(BF16)
    wg = w[:, o_g:].astype(BF16)
    wqup = jnp.pad(w_q_up[0].reshape(Q_LORA, B_HEADS, B_NOPE + B_ROPE),
                   ((0, 0), (0, 0), (0, LANES - B_NOPE - B_ROPE))).reshape(Q_LORA, B_HEADS * LANES)
    wqup = wqup.astype(BF16)
    wkvup = w_kv_up[0].astype(BF16)
    row = lambda v: v.reshape(1, -1).astype(F32)
    tables = _rope_tables(positions)
    x2d = x.reshape(t, d)

    tm = TM_PROJ
    assert tm == TK, "one projection tile is one attention key block (transposed value tiles)"
    tok = lambda width: pl.BlockSpec((tm, width), lambda i: (i, 0))
    bf = lambda width: jax.ShapeDtypeStruct((t, width), BF16)
    tok_t = lambda width: pl.BlockSpec((None, width, tm), lambda i: (i, 0, 0))
    bf_t = lambda width: jax.ShapeDtypeStruct((t // tm, width, tm), BF16)
    qa, ka, vat, qb, kvb, kvbt, kpe, sg = pl.pallas_call(
        _inproj_kernel,
        grid=(t // tm,),
        in_specs=[tok(d), _full((1, d)), _full(wqkv.shape), _full(wc.shape), _full(wg.shape),
                  _full(wqup.shape), _full(wkvup.shape), _full((1, Q_LORA)), _full((1, KV_LORA))]
                 + [tok(LANES)] * 6,
        out_specs=[tok(A_WIDTH), tok(A_WIDTH), tok_t(A_WIDTH), tok(B_HEADS * LANES),
                   tok(B_HEADS * LANES), tok_t(B_HEADS * LANES), tok(LANES), tok(2 * d)],
        out_shape=[bf(A_WIDTH), bf(A_WIDTH), bf_t(A_WIDTH), bf(B_HEADS * LANES),
                   bf(B_HEADS * LANES), bf_t(B_HEADS * LANES), bf(LANES), bf(2 * d)],
        compiler_params=_params("parallel"),
        name="inproj",
    )(x2d, row(mix_norm_g[0]), wqkv, wc, wg, wqup, wkvup, row(q_norm_g[0]), row(kv_norm_g[0]),
      *tables)

    r3 = lambda a: a.reshape(bsz, seq, a.shape[-1])
    r4t = lambda a: a.reshape(bsz, seq // tm, a.shape[1], tm)
    nblk = seq // TK
    assert nblk <= LANES
    pairs = A_WIDTH // LANES
    ya = pl.pallas_call(
        _moba_kernel,
        grid=(bsz, pairs, seq // TQA),
        in_specs=[pl.BlockSpec((None, TQA, LANES), lambda b, g, i: (b, i, g)),
                  pl.BlockSpec((None, seq, LANES), lambda b, g, i: (b, 0, g)),
                  pl.BlockSpec((None, nblk, LANES, TK), lambda b, g, i: (b, 0, g, 0))],
        out_specs=pl.BlockSpec((None, TQA, LANES), lambda b, g, i: (b, i, g)),
        out_shape=jax.ShapeDtypeStruct((bsz, seq, A_WIDTH), BF16),
        scratch_shapes=[pltpu.VMEM((LANES, LANES), BF16), pltpu.VMEM((LANES, LANES), BF16),
                        pltpu.VMEM((2, LANES, TQA), F32)],
        compiler_params=_params("parallel", "parallel", "arbitrary"),
        name="moba",
    )(r3(qa), r3(ka), r4t(vat))

    yb = pl.pallas_call(
        _mla_kernel,
        grid=(bsz, B_HEADS // 2, seq // TQA),
        in_specs=[pl.BlockSpec((None, TQA, 2 * LANES), lambda b, g, i: (b, i, g)),
                  pl.BlockSpec((None, seq, 2 * LANES), lambda b, g, i: (b, 0, g)),
                  pl.BlockSpec((None, seq, LANES), lambda b, g, i: (b, 0, 0)),
                  pl.BlockSpec((None, nblk, 2 * LANES, TK), lambda b, g, i: (b, 0, g, 0))],
        out_specs=pl.BlockSpec((None, TQA, LANES), lambda b, g, i: (b, i, g)),
        out_shape=jax.ShapeDtypeStruct((bsz, seq, B_HEADS * B_V), BF16),
        scratch_shapes=[pltpu.VMEM((2, LANES, TQA), F32)],
        compiler_params=_params("parallel", "parallel", "arbitrary"),
        name="mla",
    )(r3(qb), r3(kvb), r3(kpe), r4t(kvbt))

    wba = w_branch_a[0].astype(BF16)
    wbb = w_branch_b[0].astype(BF16)
    wout = w_out[0].astype(BF16)
    wpq = w_peer_query[0].astype(BF16)
    sk1 = peer_sub_keys_1[0].astype(BF16)
    sk2 = peer_sub_keys_2[0].astype(BF16)
    score_spec = pl.BlockSpec((PEER_HEADS, PEER_NKEYS, tm), lambda i: (0, 0, i))
    score_shape = jax.ShapeDtypeStruct((PEER_HEADS, PEER_NKEYS, t), F32)
    x1, h2, s1t, s2t = pl.pallas_call(
        _merge_kernel,
        grid=(t // tm,),
        in_specs=[tok(d), tok(A_WIDTH), tok(B_HEADS * B_V), tok(2 * d), _full(wba.shape),
                  _full(wbb.shape), _full(wout.shape), _full((1, d)), _full(wpq.shape),
                  _full(sk1.shape), _full(sk2.shape)],
        out_specs=[tok(d), tok(d), score_spec, score_spec],
        out_shape=[jax.ShapeDtypeStruct((t, d), F32), bf(d), score_shape, score_shape],
        compiler_params=_params("parallel"),
        name="merge",
    )(x2d, ya.reshape(t, A_WIDTH), yb.reshape(t, B_HEADS * B_V), sg, wba, wbb, wout,
      row(ffn_norm_g[0]), wpq, sk1, sk2)

    tl = TL_ROUTE
    hs = pl.BlockSpec((PEER_HEADS, PEER_NKEYS, tl), lambda i: (0, 0, i))
    a_co, e2, tau = pl.pallas_call(
        _route_kernel,
        grid=(t // tl,),
        in_specs=[hs, hs],
        out_specs=[hs, hs, pl.BlockSpec((PEER_HEADS, 1, tl), lambda i: (0, 0, i))],
        out_shape=[score_shape, score_shape, jax.ShapeDtypeStruct((PEER_HEADS, 1, t), F32)],
        scratch_shapes=[pltpu.VMEM((PEER_TOPK, tl), F32)] * 3,
        compiler_params=_params("parallel"),
        name="route",
    )(s1t, s2t)

    u_bf = peer_expert_u[0].astype(BF16)
    vt_bf = peer_expert_v[0].T.astype(BF16)
    tt, te = TT_EXP, TE_EXP
    rt = pl.BlockSpec((PEER_HEADS, PEER_NKEYS, tt), lambda i, e: (0, 0, i))
    n_tiles = n_exp // te
    out = pl.pallas_call(
        _expert_kernel,
        grid=(t // tt, n_tiles + 1),
        in_specs=[pl.BlockSpec((tt, d), lambda i, e: (i, 0)),
                  pl.BlockSpec((te, d), lambda i, e: (jnp.minimum(e, n_tiles - 1), 0)),
                  pl.BlockSpec((d, te), lambda i, e: (0, jnp.maximum(e - 1, 0))),
                  rt, rt, rt, rt,
                  pl.BlockSpec((PEER_HEADS, 1, tt), lambda i, e: (0, 0, i)),
                  pl.BlockSpec((tt, d), lambda i, e: (i, 0)),
                  pl.BlockSpec((1, d), lambda i, e: (0, 0))],
        out_specs=pl.BlockSpec((tt, d), lambda i, e: (i, 0)),
        out_shape=jax.ShapeDtypeStruct((t, d), F32),
        scratch_shapes=[pltpu.VMEM((d, tt), F32), pltpu.VMEM((te, tt), F32),
                        pltpu.VMEM((te, tt), F32), pltpu.VMEM((te, tt), BF16)],
        compiler_params=_params("parallel", "arbitrary"),
        name="experts",
    )(h2, u_bf, vt_bf, s1t, s2t, a_co, e2, tau, x1, row(final_norm_g))
    return out.reshape(bsz, seq, d)
```

```python
import functools
import math

import jax
import jax.numpy as jnp
from jax import lax
from jax.experimental import pallas as pl
from jax.experimental.pallas import tpu as pltpu

F32 = jnp.float32
BF16 = jnp.bfloat16

D_MODEL = 1024
A_HEADS = 8
A_HEAD_DIM = 64
A_WIDTH = A_HEADS * A_HEAD_DIM
MOBA_BLOCK = 256
MOBA_TOPK = 3
B_HEADS = 8
B_NOPE = 64
B_ROPE = 32
B_V = 64
Q_LORA = 256
KV_LORA = 128
PEER_HEADS = 8
PEER_NKEYS = 128
PEER_HALF = 128
PEER_TOPK = 16
ROPE_THETA = 10000.0
RMS_EPS = 1e-6

LANES = 128
NEG_BIG = -1e30
LOG2E = math.log2(math.e)
VMEM_LIMIT = 56 * 1024 * 1024

TM_PROJ = 256
TK = MOBA_BLOCK
TQA = 8 * TK
TL_ROUTE = 256
TT_EXP = 512
TE_EXP = 512
TB_EXP = 256
TJ_EXP = 32


def _dot(a, b):
    return jnp.dot(a, b, preferred_element_type=F32)


def _dot_nt(a, b):
    return lax.dot_general(a, b, (((1,), (1,)), ((), ())), preferred_element_type=F32)


def _rms(x, g):
    return x * lax.rsqrt(jnp.mean(x * x, axis=-1, keepdims=True) + RMS_EPS) * g


def _inproj_kernel(x_ref, g_ref, wqkv_ref, wc_ref, wg_ref, wqup_ref, wkvup_ref, qng_ref,
                   kvng_ref, ca_ref, sa1_ref, sa2_ref, cb_ref, sb1_ref, sb2_ref,
                   qa_ref, ka_ref, vat_ref, qb_ref, kvb_ref, kvbt_ref, kpe_ref, sg_ref):
    h = _rms(x_ref[...], g_ref[...]).astype(BF16)

    qkv = _dot(h, wqkv_ref[...])
    ca, sa1, sa2 = ca_ref[...], sa1_ref[...], sa2_ref[...]

    def rope_a(xg):
        return xg * ca + pltpu.roll(xg, LANES - 32, 1) * sa1 + pltpu.roll(xg, 32, 1) * sa2

    for grp in range(A_WIDTH // LANES):
        lo = grp * LANES
        qa_ref[:, lo:lo + LANES] = (rope_a(qkv[:, lo:lo + LANES]) * (0.125 * LOG2E)).astype(BF16)
        ka_ref[:, lo:lo + LANES] = rope_a(qkv[:, A_WIDTH + lo:A_WIDTH + lo + LANES]).astype(BF16)
    vat_ref[...] = qkv[:, 2 * A_WIDTH:].T.astype(BF16)

    c = _dot(h, wc_ref[...])
    cb, sb1, sb2 = cb_ref[...], sb1_ref[...], sb2_ref[...]

    def rope_b(xg):
        return xg * cb + pltpu.roll(xg, LANES - 16, 1) * sb1 + pltpu.roll(xg, 16, 1) * sb2

    cq = _rms(c[:, :Q_LORA], qng_ref[...]).astype(BF16)
    qb = _dot(cq, wqup_ref[...]) * (LOG2E / math.sqrt(B_NOPE + B_ROPE))
    for hd in range(B_HEADS):
        lo = hd * LANES
        qb_ref[:, lo:lo + LANES] = rope_b(qb[:, lo:lo + LANES]).astype(BF16)
    ckv = _rms(c[:, Q_LORA:Q_LORA + KV_LORA], kvng_ref[...]).astype(BF16)
    kv = _dot(ckv, wkvup_ref[...])
    kvb_ref[...] = kv.astype(BF16)
    kvbt_ref[...] = kv.T.astype(BF16)
    kpe_ref[...] = rope_b(c[:, Q_LORA + KV_LORA:]).astype(BF16)

    sg_ref[...] = jax.nn.sigmoid(_dot(h, wg_ref[...])).astype(BF16)


def _attn_step(scores, carry, acc_ref, vts, c0=0):
    new, ps, alphas = [], [], []
    for s, (m, l) in zip(scores, carry):
        m_new = jnp.maximum(m[:, c0:], jnp.max(s, axis=0, keepdims=True))
        alpha = jnp.exp2(m[:, c0:] - m_new)
        p = jnp.exp2(s - m_new)
        l_new = alpha * l[:, c0:] + jnp.sum(p, axis=0, keepdims=True)
        if c0:
            m_new = jnp.concatenate([m[:, :c0], m_new], axis=1)
            l_new = jnp.concatenate([l[:, :c0], l_new], axis=1)
        new.append((m_new, l_new))
        ps.append(p.astype(BF16))
        alphas.append(alpha)
    for hh in range(2):
        acc_ref[hh, :, c0:] = alphas[hh] * acc_ref[hh, :, c0:] + _dot(vts[hh], ps[hh])
    return tuple(new)


def _attn_loop(i, step):
    per = TQA // TK
    init = tuple((jnp.full((1, TQA), NEG_BIG, F32), jnp.zeros((1, TQA), F32)) for _ in range(2))
    carry = lax.fori_loop(0, per * i, lambda j, c: step(j, c, None, 0), init)
    for a in range(per):
        c0 = a * TK
        krow = lax.broadcasted_iota(jnp.int32, (TK, TQA - c0), 0)
        qcol = lax.broadcasted_iota(jnp.int32, (TK, TQA - c0), 1)
        carry = step(per * i + a, carry, krow <= qcol, c0)
    return carry


def _transpose_bf16(x):
    return x.astype(F32).T.astype(BF16)


def _moba_kernel(q_ref, k_ref, vt_ref, o_ref, kmh_ref, kml_ref, acc_ref):
    i = pl.program_id(2)
    nblk = k_ref.shape[0] // TK

    @pl.when(i == 0)
    def _():
        rows = lax.broadcasted_iota(jnp.int32, (LANES, LANES), 0)
        km = jnp.zeros((LANES, LANES), F32)
        for j in range(nblk):
            blk = k_ref[j * TK:(j + 1) * TK, :].astype(F32)
            km = jnp.where(rows == j, jnp.mean(blk, axis=0, keepdims=True), km)
        hi = km.astype(BF16)
        kmh_ref[...] = hi
        kml_ref[...] = (km - hi.astype(F32)).astype(BF16)

    qt = _transpose_bf16(q_ref[...])
    sub = lax.broadcasted_iota(jnp.int32, (LANES, TQA), 0)
    col = lax.broadcasted_iota(jnp.int32, (LANES, TQA), 1)
    own = (TQA // TK) * i + col // TK
    zero = jnp.zeros_like(qt)
    qt_heads = (jnp.where(sub < A_HEAD_DIM, qt, zero), jnp.where(sub >= A_HEAD_DIM, qt, zero))

    q_aug = []
    for qh in qt_heads:
        gate = _dot(kmh_ref[...], qh) + _dot(kml_ref[...], qh)
        valid = sub < own
        g = jnp.where(valid, gate, -jnp.inf)[:nblk]
        thr = jnp.max(g, axis=0, keepdims=True)
        for _ in range(MOBA_TOPK - 1):
            g = jnp.where(g >= thr, -jnp.inf, g)
            thr = jnp.max(g, axis=0, keepdims=True)
        keep = (valid & (gate >= thr)) | (sub == own)
        bias = jnp.where(keep, 0.0, NEG_BIG).astype(BF16)
        q_aug.append(jnp.concatenate([qh, bias], axis=0))

    lane = lax.broadcasted_iota(jnp.int32, (TK, LANES), 1)
    acc_ref[...] = jnp.zeros_like(acc_ref)

    def step(j, carry, causal, c0):
        start = pl.multiple_of(j * TK, TK)
        onehot = jnp.where(lane == j, 1.0, 0.0).astype(BF16)
        ka = jnp.concatenate([k_ref[pl.ds(start, TK), :], onehot], axis=1)
        scores = [_dot(ka, qa[:, c0:]) for qa in q_aug]
        if causal is not None:
            scores = [jnp.where(causal, s, NEG_BIG) for s in scores]
        vt = vt_ref[j]
        return _attn_step(scores, carry, acc_ref, (vt, vt), c0)

    carry = _attn_loop(i, step)
    outs = [acc_ref[hh] / carry[hh][1] for hh in range(2)]
    o_ref[...] = jnp.where(sub < A_HEAD_DIM, outs[0], outs[1]).T.astype(o_ref.dtype)


def _mla_kernel(q_ref, kv_ref, kpe_ref, kvt_ref, o_ref, acc_ref):
    i = pl.program_id(2)
    lane = lax.broadcasted_iota(jnp.int32, (TK, LANES), 1)
    qts = (_transpose_bf16(q_ref[:, :LANES]), _transpose_bf16(q_ref[:, LANES:]))
    acc_ref[...] = jnp.zeros_like(acc_ref)

    def step(j, carry, causal, c0):
        start = pl.multiple_of(j * TK, TK)
        kpe = kpe_ref[pl.ds(start, TK), :]
        scores = []
        for hh in range(2):
            kv = kv_ref[pl.ds(start, TK), hh * LANES:(hh + 1) * LANES]
            key = jnp.where(lane < B_NOPE, kv, kpe)
            scores.append(_dot(key, qts[hh][:, c0:]))
        if causal is not None:
            scores = [jnp.where(causal, s, NEG_BIG) for s in scores]
        return _attn_step(scores, carry, acc_ref,
                          (kvt_ref[j, :LANES, :], kvt_ref[j, LANES:, :]), c0)

    carry = _attn_loop(i, step)
    outs = [(acc_ref[hh] / carry[hh][1])[B_NOPE:, :] for hh in range(2)]
    o_ref[...] = jnp.concatenate(outs, axis=0).T.astype(o_ref.dtype)


def _merge_kernel(x_ref, ya_ref, yb_ref, sg_ref, wba_ref, wbb_ref, wout_ref, fg_ref, wpq_ref,
                  sk1_ref, sk2_ref, x1_ref, h2_ref, s1_ref, s2_ref):
    merged = (sg_ref[:, :D_MODEL].astype(F32) * _dot(ya_ref[...], wba_ref[...])
              + sg_ref[:, D_MODEL:].astype(F32) * _dot(yb_ref[...], wbb_ref[...]))
    x1 = x_ref[...] + _dot(merged.astype(BF16), wout_ref[...])
    x1_ref[...] = x1
    h2 = _rms(x1, fg_ref[...]).astype(BF16)
    h2_ref[...] = h2
    qp = _dot(h2, wpq_ref[...]).astype(BF16)
    for hd in range(PEER_HEADS):
        lo = 2 * hd * PEER_HALF
        s1_ref[hd] = _dot_nt(sk1_ref[hd], qp[:, lo:lo + PEER_HALF])
        s2_ref[hd] = _dot_nt(sk2_ref[hd], qp[:, lo + PEER_HALF:lo + 2 * PEER_HALF])


def _route_kernel(s1_ref, s2_ref, a_ref, e2_ref, tau_ref, v1_ref, v2_ref, best_ref):
    def top16(src, dst_ref):
        cur = src
        for r in range(PEER_TOPK):
            mx = jnp.max(cur, axis=0, keepdims=True)
            dst_ref[r:r + 1, :] = mx
            cur = jnp.where(cur == mx, -jnp.inf, cur)

    def head(hd, carry):
        s1 = s1_ref[hd]
        s2 = s2_ref[hd]
        top16(s1, v1_ref)
        top16(s2, v2_ref)
        v1 = v1_ref[...]
        v2 = v2_ref[...]
        half = PEER_TOPK // 2
        cand = jnp.concatenate(
            [v1 + v2[0:1]] + [v1[:half] + v2[r:r + 1] for r in range(1, half)]
            + [v1[0:1] + v2[half:]], axis=0)
        top16(cand, best_ref)
        best = best_ref[...]
        z = jnp.sum(jnp.exp(best - best[0:1, :]), axis=0, keepdims=True)
        tau_ref[hd] = best[PEER_TOPK - 1:PEER_TOPK, :]
        a_ref[hd] = jnp.exp(s1 - v1_ref[0:1, :]) * (0.5 / z)
        e2_ref[hd] = jnp.exp(s2 - v2_ref[0:1, :])
        return carry

    lax.fori_loop(0, PEER_HEADS, head, 0)


def _expert_kernel(h2_ref, u_ref, vt_ref, s1_ref, s2_ref, a_ref, e2_ref, tau_ref, x1_ref, fg_ref,
                   o_ref, acc_ref, act0_ref, act1_ref, wa_ref, *, n_tiles):
    e = pl.program_id(1)
    slabs = TE_EXP // PEER_NKEYS
    prev = jnp.maximum(e - 1, 0)

    def body(act_w_ref, act_r_ref):
        for tb in range(TT_EXP // TB_EXP):
            ts = slice(tb * TB_EXP, (tb + 1) * TB_EXP)
            if act_w_ref is not None:
                pre = _dot_nt(u_ref[...], h2_ref[ts, :])
                act_w_ref[:, ts] = pre * (1.0 + lax.erf(pre * math.sqrt(0.5)))
            if act_r_ref is None:
                continue
            tau = [tau_ref[hd, :, ts] for hd in range(PEER_HEADS)]
            for sl in range(slabs):
                i1 = prev * slabs + sl
                s1r = [s1_ref[hd, pl.ds(i1, 1), ts] for hd in range(PEER_HEADS)]
                ar = [a_ref[hd, pl.ds(i1, 1), ts] for hd in range(PEER_HEADS)]
                for jt in range(PEER_NKEYS // TJ_EXP):
                    rj = slice(jt * TJ_EXP, (jt + 1) * TJ_EXP)
                    w = None
                    for hd in range(PEER_HEADS):
                        cand = s2_ref[hd, rj, ts] + s1r[hd]
                        term = jnp.where(cand >= tau[hd], e2_ref[hd, rj, ts], 0.0) * ar[hd]
                        w = term if w is None else w + term
                    ro = slice(sl * PEER_NKEYS + jt * TJ_EXP, sl * PEER_NKEYS + (jt + 1) * TJ_EXP)
                    wa_ref[ro, ts] = (w * act_r_ref[ro, ts]).astype(BF16)
            acc_ref[:, ts] += _dot(vt_ref[...], wa_ref[:, ts])

    acts = (act0_ref, act1_ref)
    inner = jnp.logical_and(e > 0, e < n_tiles)

    @pl.when(e == 0)
    def _():
        acc_ref[...] = jnp.zeros_like(acc_ref)
        body(acts[0], None)

    @pl.when(jnp.logical_and(inner, e % 2 == 0))
    def _():
        body(acts[0], acts[1])

    @pl.when(jnp.logical_and(inner, e % 2 == 1))
    def _():
        body(acts[1], acts[0])

    @pl.when(e == n_tiles)
    def _():
        body(None, acts[(n_tiles - 1) % 2])
        o_ref[...] = _rms(x1_ref[...] + acc_ref[...].T, fg_ref[...])


def _params(*sem):
    return pltpu.CompilerParams(dimension_semantics=sem, vmem_limit_bytes=VMEM_LIMIT)


def _full(shape):
    nd = len(shape)
    return pl.BlockSpec(shape, lambda *_: (0,) * nd)


def _rope_tables(positions):
    pos = positions.reshape(-1).astype(F32)[:, None]

    def cs(dim):
        inv_freq = ROPE_THETA ** (-jnp.arange(0, dim, 2, dtype=F32) / dim)
        ang = pos * inv_freq
        return jnp.cos(ang), jnp.sin(ang)

    ca, sa = cs(A_HEAD_DIM)
    za = jnp.zeros_like(sa)
    cb, sb = cs(B_ROPE)
    t = pos.shape[0]
    zb = lambda n: jnp.zeros((t, n), F32)
    ob = lambda n: jnp.ones((t, n), F32)
    return (jnp.concatenate([ca] * 4, axis=1),
            jnp.concatenate([-sa, za, -sa, za], axis=1),
            jnp.concatenate([za, sa, za, sa], axis=1),
            jnp.concatenate([ob(B_NOPE), cb, cb, ob(32)], axis=1),
            jnp.concatenate([zb(B_NOPE), -sb, zb(16), zb(32)], axis=1),
            jnp.concatenate([zb(B_NOPE), zb(16), sb, zb(32)], axis=1))


def kernel(x, positions, mix_norm_g, w_in, q_norm_g, w_q_up, kv_norm_g, w_kv_up, w_branch_a,
           w_branch_b, w_out, ffn_norm_g, w_peer_query, peer_sub_keys_1, peer_sub_keys_2,
           peer_expert_u, peer_expert_v, final_norm_g):
    bsz, seq, d = x.shape
    t = bsz * seq
    n_exp = peer_expert_u.shape[1]
    assert d == D_MODEL and seq % TQA == 0 and t % TT_EXP == 0 and n_exp % TE_EXP == 0
    assert mix_norm_g.shape[0] == 1, "single-layer block"

    w = w_in[0]
    o_cq = 3 * A_WIDTH
    o_ckv = o_cq + Q_LORA
    o_kpe = o_ckv + KV_LORA
    o_g = o_kpe + B_ROPE
    wqkv = w[:, :o_cq].astype(BF16)
    wc = jnp.concatenate([w[:, o_cq:o_kpe], jnp.zeros((d, B_NOPE), F32), w[:, o_kpe:o_g],
                          jnp.zeros((d, LANES - B_NOPE - B_ROPE), F32)], axis=1).astype(BF16)
    wg = w[:, o_g:].astype(BF16)
    wqup = jnp.pad(w_q_up[0].reshape(Q_LORA, B_HEADS, B_NOPE + B_ROPE),
                   ((0, 0), (0, 0), (0, LANES - B_NOPE - B_ROPE))).reshape(Q_LORA, B_HEADS * LANES)
    wqup = wqup.astype(BF16)
    wkvup = w_kv_up[0].astype(BF16)
    row = lambda v: v.reshape(1, -1).astype(F32)
    tables = _rope_tables(positions)
    x2d = x.reshape(t, d)

    tm = TM_PROJ
    assert tm == TK, "one projection tile is one attention key block (transposed value tiles)"
    tok = lambda width: pl.BlockSpec((tm, width), lambda i: (i, 0))
    bf = lambda width: jax.ShapeDtypeStruct((t, width), BF16)
    tok_t = lambda width: pl.BlockSpec((None, width, tm), lambda i: (i, 0, 0))
    bf_t = lambda width: jax.ShapeDtypeStruct((t // tm, width, tm), BF16)
    qa, ka, vat, qb, kvb, kvbt, kpe, sg = pl.pallas_call(
        _inproj_kernel,
        grid=(t // tm,),
        in_specs=[tok(d), _full((1, d)), _full(wqkv.shape), _full(wc.shape), _full(wg.shape),
                  _full(wqup.shape), _full(wkvup.shape), _full((1, Q_LORA)), _full((1, KV_LORA))]
                 + [tok(LANES)] * 6,
        out_specs=[tok(A_WIDTH), tok(A_WIDTH), tok_t(A_WIDTH), tok(B_HEADS * LANES),
                   tok(B_HEADS * LANES), tok_t(B_HEADS * LANES), tok(LANES), tok(2 * d)],
        out_shape=[bf(A_WIDTH), bf(A_WIDTH), bf_t(A_WIDTH), bf(B_HEADS * LANES),
                   bf(B_HEADS * LANES), bf_t(B_HEADS * LANES), bf(LANES), bf(2 * d)],
        compiler_params=_params("parallel"),
        name="inproj",
    )(x2d, row(mix_norm_g[0]), wqkv, wc, wg, wqup, wkvup, row(q_norm_g[0]), row(kv_norm_g[0]),
      *tables)

    r3 = lambda a: a.reshape(bsz, seq, a.shape[-1])
    r4t = lambda a: a.reshape(bsz, seq // tm, a.shape[1], tm)
    nblk = seq // TK
    assert nblk <= LANES
    pairs = A_WIDTH // LANES
    ya = pl.pallas_call(
        _moba_kernel,
        grid=(bsz, pairs, seq // TQA),
        in_specs=[pl.BlockSpec((None, TQA, LANES), lambda b, g, i: (b, i, g)),
                  pl.BlockSpec((None, seq, LANES), lambda b, g, i: (b, 0, g)),
                  pl.BlockSpec((None, nblk, LANES, TK), lambda b, g, i: (b, 0, g, 0))],
        out_specs=pl.BlockSpec((None, TQA, LANES), lambda b, g, i: (b, i, g)),
        out_shape=jax.ShapeDtypeStruct((bsz, seq, A_WIDTH), BF16),
        scratch_shapes=[pltpu.VMEM((LANES, LANES), BF16), pltpu.VMEM((LANES, LANES), BF16),
                        pltpu.VMEM((2, LANES, TQA), F32)],
        compiler_params=_params("parallel", "parallel", "arbitrary"),
        name="moba",
    )(r3(qa), r3(ka), r4t(vat))

    yb = pl.pallas_call(
        _mla_kernel,
        grid=(bsz, B_HEADS // 2, seq // TQA),
        in_specs=[pl.BlockSpec((None, TQA, 2 * LANES), lambda b, g, i: (b, i, g)),
                  pl.BlockSpec((None, seq, 2 * LANES), lambda b, g, i: (b, 0, g)),
                  pl.BlockSpec((None, seq, LANES), lambda b, g, i: (b, 0, 0)),
                  pl.BlockSpec((None, nblk, 2 * LANES, TK), lambda b, g, i: (b, 0, g, 0))],
        out_specs=pl.BlockSpec((None, TQA, LANES), lambda b, g, i: (b, i, g)),
        out_shape=jax.ShapeDtypeStruct((bsz, seq, B_HEADS * B_V), BF16),
        scratch_shapes=[pltpu.VMEM((2, LANES, TQA), F32)],
        compiler_params=_params("parallel", "parallel", "arbitrary"),
        name="mla",
    )(r3(qb), r3(kvb), r3(kpe), r4t(kvbt))

    wba = w_branch_a[0].astype(BF16)
    wbb = w_branch_b[0].astype(BF16)
    wout = w_out[0].astype(BF16)
    wpq = w_peer_query[0].astype(BF16)
    sk1 = peer_sub_keys_1[0].astype(BF16)
    sk2 = peer_sub_keys_2[0].astype(BF16)
    score_spec = pl.BlockSpec((PEER_HEADS, PEER_NKEYS, tm), lambda i: (0, 0, i))
    score_shape = jax.ShapeDtypeStruct((PEER_HEADS, PEER_NKEYS, t), F32)
    x1, h2, s1t, s2t = pl.pallas_call(
        _merge_kernel,
        grid=(t // tm,),
        in_specs=[tok(d), tok(A_WIDTH), tok(B_HEADS * B_V), tok(2 * d), _full(wba.shape),
                  _full(wbb.shape), _full(wout.shape), _full((1, d)), _full(wpq.shape),
                  _full(sk1.shape), _full(sk2.shape)],
        out_specs=[tok(d), tok(d), score_spec, score_spec],
        out_shape=[jax.ShapeDtypeStruct((t, d), F32), bf(d), score_shape, score_shape],
        compiler_params=_params("parallel"),
        name="merge",
    )(x2d, ya.reshape(t, A_WIDTH), yb.reshape(t, B_HEADS * B_V), sg, wba, wbb, wout,
      row(ffn_norm_g[0]), wpq, sk1, sk2)

    tl = TL_ROUTE
    hs = pl.BlockSpec((PEER_HEADS, PEER_NKEYS, tl), lambda i: (0, 0, i))
    a_co, e2, tau = pl.pallas_call(
        _route_kernel,
        grid=(t // tl,),
        in_specs=[hs, hs],
        out_specs=[hs, hs, pl.BlockSpec((PEER_HEADS, 1, tl), lambda i: (0, 0, i))],
        out_shape=[score_shape, score_shape, jax.ShapeDtypeStruct((PEER_HEADS, 1, t), F32)],
        scratch_shapes=[pltpu.VMEM((PEER_TOPK, tl), F32)] * 3,
        compiler_params=_params("parallel"),
        name="route",
    )(s1t, s2t)

    u_bf = peer_expert_u[0].astype(BF16)
    vt_bf = peer_expert_v[0].T.astype(BF16)
    tt, te = TT_EXP, TE_EXP
    rt = pl.BlockSpec((PEER_HEADS, PEER_NKEYS, tt), lambda i, e: (0, 0, i))
    n_tiles = n_exp // te
    out = pl.pallas_call(
        functools.partial(_expert_kernel, n_tiles=n_tiles),
        grid=(t // tt, n_tiles + 1),
        in_specs=[pl.BlockSpec((tt, d), lambda i, e: (i, 0)),
                  pl.BlockSpec((te, d), lambda i, e: (jnp.minimum(e, n_tiles - 1), 0)),
                  pl.BlockSpec((d, te), lambda i, e: (0, jnp.maximum(e - 1, 0))),
                  rt, rt, rt, rt,
                  pl.BlockSpec((PEER_HEADS, 1, tt), lambda i, e: (0, 0, i)),
                  pl.BlockSpec((tt, d), lambda i, e: (i, 0)),
                  pl.BlockSpec((1, d), lambda i, e: (0, 0))],
        out_specs=pl.BlockSpec((tt, d), lambda i, e: (i, 0)),
        out_shape=jax.ShapeDtypeStruct((t, d), F32),
        scratch_shapes=[pltpu.VMEM((d, tt), F32), pltpu.VMEM((te, tt), F32),
                        pltpu.VMEM((te, tt), F32), pltpu.VMEM((te, tt), BF16)],
        compiler_params=_params("parallel", "arbitrary"),
        name="experts",
    )(h2, u_bf, vt_bf, s1t, s2t, a_co, e2, tau, x1, row(final_norm_g))
    return out.reshape(bsz, seq, d)
```

```python
import functools
import math

import jax
import jax.numpy as jnp
from jax import lax
from jax.experimental import pallas as pl
from jax.experimental.pallas import tpu as pltpu

F32 = jnp.float32
BF16 = jnp.bfloat16

D_MODEL = 1024
A_HEADS = 8
A_HEAD_DIM = 64
A_WIDTH = A_HEADS * A_HEAD_DIM
MOBA_BLOCK = 256
MOBA_TOPK = 3
B_HEADS = 8
B_NOPE = 64
B_ROPE = 32
B_V = 64
Q_LORA = 256
KV_LORA = 128
PEER_HEADS = 8
PEER_NKEYS = 128
PEER_HALF = 128
PEER_TOPK = 16
ROPE_THETA = 10000.0
RMS_EPS = 1e-6

LANES = 128
NEG_BIG = -1e30
LOG2E = math.log2(math.e)
VMEM_LIMIT = 56 * 1024 * 1024

TM_PROJ = 256
TK = MOBA_BLOCK
TQA = 8 * TK
TL_ROUTE = 256
TT_EXP = 512
TE_EXP = 512
TB_EXP = 256
TJ_EXP = 32


def _dot(a, b):
    return jnp.dot(a, b, preferred_element_type=F32)


def _dot_nt(a, b):
    return lax.dot_general(a, b, (((1,), (1,)), ((), ())), preferred_element_type=F32)


def _rms(x, g):
    return x * lax.rsqrt(jnp.mean(x * x, axis=-1, keepdims=True) + RMS_EPS) * g


def _inproj_kernel(x_ref, g_ref, wqkv_ref, wc_ref, wg_ref, wqup_ref, wkvup_ref, qng_ref,
                   kvng_ref, ca_ref, sa1_ref, sa2_ref, cb_ref, sb1_ref, sb2_ref,
                   qa_ref, ka_ref, vat_ref, qb_ref, kvb_ref, kvbt_ref, kpe_ref, sg_ref):
    h = _rms(x_ref[...], g_ref[...]).astype(BF16)

    qkv = _dot(h, wqkv_ref[...])
    ca, sa1, sa2 = ca_ref[...], sa1_ref[...], sa2_ref[...]

    def rope_a(xg):
        return xg * ca + pltpu.roll(xg, LANES - 32, 1) * sa1 + pltpu.roll(xg, 32, 1) * sa2

    for grp in range(A_WIDTH // LANES):
        lo = grp * LANES
        qa_ref[:, lo:lo + LANES] = (rope_a(qkv[:, lo:lo + LANES]) * (0.125 * LOG2E)).astype(BF16)
        ka_ref[:, lo:lo + LANES] = rope_a(qkv[:, A_WIDTH + lo:A_WIDTH + lo + LANES]).astype(BF16)
    vat_ref[...] = qkv[:, 2 * A_WIDTH:].T.astype(BF16)

    c = _dot(h, wc_ref[...])
    cb, sb1, sb2 = cb_ref[...], sb1_ref[...], sb2_ref[...]

    def rope_b(xg):
        return xg * cb + pltpu.roll(xg, LANES - 16, 1) * sb1 + pltpu.roll(xg, 16, 1) * sb2

    cq = _rms(c[:, :Q_LORA], qng_ref[...]).astype(BF16)
    qb = _dot(cq, wqup_ref[...]) * (LOG2E / math.sqrt(B_NOPE + B_ROPE))
    for hd in range(B_HEADS):
        lo = hd * LANES
        qb_ref[:, lo:lo + LANES] = rope_b(qb[:, lo:lo + LANES]).astype(BF16)
    ckv = _rms(c[:, Q_LORA:Q_LORA + KV_LORA], kvng_ref[...]).astype(BF16)
    kv = _dot(ckv, wkvup_ref[...])
    kvb_ref[...] = kv.astype(BF16)
    kvbt_ref[...] = kv.T.astype(BF16)
    kpe_ref[...] = rope_b(c[:, Q_LORA + KV_LORA:]).astype(BF16)

    sg_ref[...] = jax.nn.sigmoid(_dot(h, wg_ref[...])).astype(BF16)


def _attn_step(scores, carry, acc_ref, vts, c0=0):
    new, ps, alphas = [], [], []
    for s, (m, l) in zip(scores, carry):
        m_new = jnp.maximum(m[:, c0:], jnp.max(s, axis=0, keepdims=True))
        alpha = jnp.exp2(m[:, c0:] - m_new)
        p = jnp.exp2(s - m_new)
        l_new = alpha * l[:, c0:] + jnp.sum(p, axis=0, keepdims=True)
        if c0:
            m_new = jnp.concatenate([m[:, :c0], m_new], axis=1)
            l_new = jnp.concatenate([l[:, :c0], l_new], axis=1)
        new.append((m_new, l_new))
        ps.append(p.astype(BF16))
        alphas.append(alpha)
    for hh in range(2):
        acc_ref[hh, :, c0:] = alphas[hh] * acc_ref[hh, :, c0:] + _dot(vts[hh], ps[hh])
    return tuple(new)


def _attn_loop(i, step):
    per = TQA // TK
    init = tuple((jnp.full((1, TQA), NEG_BIG, F32), jnp.zeros((1, TQA), F32)) for _ in range(2))
    carry = lax.fori_loop(0, per * i, lambda j, c: step(j, c, None, 0), init)
    for a in range(per):
        c0 = a * TK
        krow = lax.broadcasted_iota(jnp.int32, (TK, TQA - c0), 0)
        qcol = lax.broadcasted_iota(jnp.int32, (TK, TQA - c0), 1)
        carry = step(per * i + a, carry, krow <= qcol, c0)
    return carry


def _transpose_bf16(x):
    return x.astype(F32).T.astype(BF16)


def _moba_kernel(q_ref, k_ref, vt_ref, o_ref, kmh_ref, kml_ref, acc_ref):
    i = pl.program_id(2)
    nblk = k_ref.shape[0] // TK

    @pl.when(i == 0)
    def _():
        rows = lax.broadcasted_iota(jnp.int32, (LANES, LANES), 0)
        km = jnp.zeros((LANES, LANES), F32)
        for j in range(nblk):
            blk = k_ref[j * TK:(j + 1) * TK, :].astype(F32)
            km = jnp.where(rows == j, jnp.mean(blk, axis=0, keepdims=True), km)
        hi = km.astype(BF16)
        kmh_ref[...] = hi
        kml_ref[...] = (km - hi.astype(F32)).astype(BF16)

    qt = _transpose_bf16(q_ref[...])
    sub = lax.broadcasted_iota(jnp.int32, (LANES, TQA), 0)
    col = lax.broadcasted_iota(jnp.int32, (LANES, TQA), 1)
    own = (TQA // TK) * i + col // TK
    zero = jnp.zeros_like(qt)
    qt_heads = (jnp.where(sub < A_HEAD_DIM, qt, zero), jnp.where(sub >= A_HEAD_DIM, qt, zero))

    q_aug = []
    for qh in qt_heads:
        gate = _dot(kmh_ref[...], qh) + _dot(kml_ref[...], qh)
        valid = sub < own
        g = jnp.where(valid, gate, -jnp.inf)[:nblk]
        thr = jnp.max(g, axis=0, keepdims=True)
        for _ in range(MOBA_TOPK - 1):
            g = jnp.where(g >= thr, -jnp.inf, g)
            thr = jnp.max(g, axis=0, keepdims=True)
        keep = (valid & (gate >= thr)) | (sub == own)
        bias = jnp.where(keep, 0.0, NEG_BIG).astype(BF16)
        q_aug.append(jnp.concatenate([qh, bias], axis=0))

    lane = lax.broadcasted_iota(jnp.int32, (TK, LANES), 1)
    acc_ref[...] = jnp.zeros_like(acc_ref)

    def step(j, carry, causal, c0):
        start = pl.multiple_of(j * TK, TK)
        onehot = jnp.where(lane == j, 1.0, 0.0).astype(BF16)
        ka = jnp.concatenate([k_ref[pl.ds(start, TK), :], onehot], axis=1)
        scores = [_dot(ka, qa[:, c0:]) for qa in q_aug]
        if causal is not None:
            scores = [jnp.where(causal, s, NEG_BIG) for s in scores]
        vt = vt_ref[j]
        return _attn_step(scores, carry, acc_ref, (vt, vt), c0)

    carry = _attn_loop(i, step)
    outs = [acc_ref[hh] / carry[hh][1] for hh in range(2)]
    o_ref[...] = jnp.where(sub < A_HEAD_DIM, outs[0], outs[1]).T.astype(o_ref.dtype)


def _mla_kernel(q_ref, kv_ref, kpe_ref, kvt_ref, o_ref, acc_ref):
    i = pl.program_id(2)
    lane = lax.broadcasted_iota(jnp.int32, (TK, LANES), 1)
    qts = (_transpose_bf16(q_ref[:, :LANES]), _transpose_bf16(q_ref[:, LANES:]))
    acc_ref[...] = jnp.zeros_like(acc_ref)

    def step(j, carry, causal, c0):
        start = pl.multiple_of(j * TK, TK)
        kpe = kpe_ref[pl.ds(start, TK), :]
        scores = []
        for hh in range(2):
            kv = kv_ref[pl.ds(start, TK), hh * LANES:(hh + 1) * LANES]
            key = jnp.where(lane < B_NOPE, kv, kpe)
            scores.append(_dot(key, qts[hh][:, c0:]))
        if causal is not None:
            scores = [jnp.where(causal, s, NEG_BIG) for s in scores]
        return _attn_step(scores, carry, acc_ref,
                          (kvt_ref[j, :LANES, :], kvt_ref[j, LANES:, :]), c0)

    carry = _attn_loop(i, step)
    outs = [(acc_ref[hh] / carry[hh][1])[B_NOPE:, :] for hh in range(2)]
    o_ref[...] = jnp.concatenate(outs, axis=0).T.astype(o_ref.dtype)


def _merge_kernel(x_ref, ya_ref, yb_ref, sg_ref, wba_ref, wbb_ref, wout_ref, fg_ref, wpq_ref,
                  sk1_ref, sk2_ref, x1_ref, h2_ref, s1_ref, s2_ref):
    merged = (sg_ref[:, :D_MODEL].astype(F32) * _dot(ya_ref[...], wba_ref[...])
              + sg_ref[:, D_MODEL:].astype(F32) * _dot(yb_ref[...], wbb_ref[...]))
    x1 = x_ref[...] + _dot(merged.astype(BF16), wout_ref[...])
    x1_ref[...] = x1
    h2 = _rms(x1, fg_ref[...]).astype(BF16)
    h2_ref[...] = h2
    qp = _dot(h2, wpq_ref[...]).astype(BF16)
    for hd in range(PEER_HEADS):
        lo = 2 * hd * PEER_HALF
        s1_ref[hd] = _dot_nt(sk1_ref[hd], qp[:, lo:lo + PEER_HALF])
        s2_ref[hd] = _dot_nt(sk2_ref[hd], qp[:, lo + PEER_HALF:lo + 2 * PEER_HALF])


def _route_kernel(s1_ref, s2_ref, a_ref, e2_ref, tau_ref, v1_ref, v2_ref, best_ref):
    def top16(src, dst_ref):
        cur = src
        for r in range(PEER_TOPK):
            mx = jnp.max(cur, axis=0, keepdims=True)
            dst_ref[r:r + 1, :] = mx
            cur = jnp.where(cur == mx, -jnp.inf, cur)

    def head(hd, carry):
        s1 = s1_ref[hd]
        s2 = s2_ref[hd]
        top16(s1, v1_ref)
        top16(s2, v2_ref)
        v1 = v1_ref[...]
        v2 = v2_ref[...]
        half = PEER_TOPK // 2
        cand = jnp.concatenate(
            [v1 + v2[0:1]] + [v1[:half] + v2[r:r + 1] for r in range(1, half)]
            + [v1[0:1] + v2[half:]], axis=0)
        top16(cand, best_ref)
        best = best_ref[...]
        z = jnp.sum(jnp.exp(best - best[0:1, :]), axis=0, keepdims=True)
        tau_ref[hd] = best[PEER_TOPK - 1:PEER_TOPK, :]
        a_ref[hd] = jnp.exp(s1 - v1_ref[0:1, :]) * (0.5 / z)
        e2_ref[hd] = jnp.exp(s2 - v2_ref[0:1, :])
        return carry

    lax.fori_loop(0, PEER_HEADS, head, 0)


def _expert_kernel(h2_ref, u_ref, vt_ref, s1_ref, s2_ref, a_ref, e2_ref, tau_ref, x1_ref, fg_ref,
                   o_ref, acc_ref, act0_ref, act1_ref, wa_ref, *, n_tiles):
    e = pl.program_id(1)
    slabs = TE_EXP // PEER_NKEYS
    prev = jnp.maximum(e - 1, 0)

    def body(act_w_ref, act_r_ref):
        for tb in range(TT_EXP // TB_EXP):
            ts = slice(tb * TB_EXP, (tb + 1) * TB_EXP)
            if act_w_ref is not None:
                pre = _dot_nt(u_ref[...], h2_ref[ts, :])
                act_w_ref[:, ts] = pre * (1.0 + lax.erf(pre * math.sqrt(0.5)))
            if act_r_ref is None:
                continue
            tau = [tau_ref[hd, :, ts] for hd in range(PEER_HEADS)]
            for sl in range(slabs):
                i1 = prev * slabs + sl
                s1r = [s1_ref[hd, pl.ds(i1, 1), ts] for hd in range(PEER_HEADS)]
                ar = [a_ref[hd, pl.ds(i1, 1), ts] for hd in range(PEER_HEADS)]
                for jt in range(PEER_NKEYS // TJ_EXP):
                    rj = slice(jt * TJ_EXP, (jt + 1) * TJ_EXP)
                    w = None
                    for hd in range(PEER_HEADS):
                        cand = s2_ref[hd, rj, ts] + s1r[hd]
                        term = jnp.where(cand >= tau[hd], e2_ref[hd, rj, ts], 0.0) * ar[hd]
                        w = term if w is None else w + term
                    ro = slice(sl * PEER_NKEYS + jt * TJ_EXP, sl * PEER_NKEYS + (jt + 1) * TJ_EXP)
                    wa_ref[ro, ts] = (w * act_r_ref[ro, ts]).astype(BF16)
            acc_ref[:, ts] += _dot(vt_ref[...], wa_ref[:, ts])

    acts = (act0_ref, act1_ref)
    inner = jnp.logical_and(e > 0, e < n_tiles)

    @pl.when(e == 0)
    def _():
        acc_ref[...] = jnp.zeros_like(acc_ref)
        body(acts[0], None)

    @pl.when(jnp.logical_and(inner, e % 2 == 0))
    def _():
        body(acts[0], acts[1])

    @pl.when(jnp.logical_and(inner, e % 2 == 1))
    def _():
        body(acts[1], acts[0])

    @pl.when(e == n_tiles)
    def _():
        body(None, acts[(n_tiles - 1) % 2])
        o_ref[...] = _rms(x1_ref[...] + acc_ref[...].T, fg_ref[...])


def _params(*sem):
    return pltpu.CompilerParams(dimension_semantics=sem, vmem_limit_bytes=VMEM_LIMIT)


def _full(shape):
    nd = len(shape)
    return pl.BlockSpec(shape, lambda *_: (0,) * nd)


def _rope_tables(positions):
    pos = positions.reshape(-1).astype(F32)[:, None]

    def cs(dim):
        inv_freq = ROPE_THETA ** (-jnp.arange(0, dim, 2, dtype=F32) / dim)
        ang = pos * inv_freq
        return jnp.cos(ang), jnp.sin(ang)

    t = pos.shape[0]

    def tiled(a):
        n = a.shape[1]
        return jnp.broadcast_to(a[:, None, :], (t, LANES // n, n)).reshape(t, LANES)

    def lanes(*runs):
        return jnp.asarray([[v for v, n in runs for _ in range(n)]], F32)

    ca, sa = cs(A_HEAD_DIM)
    ha = A_HEAD_DIM // 2
    cb, sb = cs(B_ROPE)
    hb = B_ROPE // 2
    rest = LANES - B_NOPE - B_ROPE
    rot = lanes((0.0, B_NOPE), (1.0, B_ROPE), (0.0, rest))
    return (tiled(ca),
            tiled(sa) * lanes((-1.0, ha), (0.0, ha), (-1.0, ha), (0.0, ha)),
            tiled(sa) * lanes((0.0, ha), (1.0, ha), (0.0, ha), (1.0, ha)),
            tiled(cb) * rot + (1.0 - rot),
            tiled(sb) * lanes((0.0, B_NOPE), (-1.0, hb), (0.0, hb), (0.0, rest)),
            tiled(sb) * lanes((0.0, B_NOPE), (0.0, hb), (1.0, hb), (0.0, rest)))


def kernel(x, positions, mix_norm_g, w_in, q_norm_g, w_q_up, kv_norm_g, w_kv_up, w_branch_a,
           w_branch_b, w_out, ffn_norm_g, w_peer_query, peer_sub_keys_1, peer_sub_keys_2,
           peer_expert_u, peer_expert_v, final_norm_g):
    bsz, seq, d = x.shape
    t = bsz * seq
    n_exp = peer_expert_u.shape[1]
    assert d == D_MODEL and seq % TQA == 0 and t % TT_EXP == 0 and n_exp % TE_EXP == 0
    assert mix_norm_g.shape[0] == 1, "single-layer block"

    w = w_in[0]
    o_cq = 3 * A_WIDTH
    o_ckv = o_cq + Q_LORA
    o_kpe = o_ckv + KV_LORA
    o_g = o_kpe + B_ROPE
    wqkv = w[:, :o_cq].astype(BF16)
    wc = jnp.concatenate([w[:, o_cq:o_kpe], jnp.zeros((d, B_NOPE), F32), w[:, o_kpe:o_g],
                          jnp.zeros((d, LANES - B_NOPE - B_ROPE), F32)], axis=1).astype(BF16)
    wg = w[:, o_g:].astype(BF16)
    wqup = jnp.pad(w_q_up[0].reshape(Q_LORA, B_HEADS, B_NOPE + B_ROPE),
                   ((0, 0), (0, 0), (0, LANES - B_NOPE - B_ROPE))).reshape(Q_LORA, B_HEADS * LANES)
    wqup = wqup.astype(BF16)
    wkvup = w_kv_up[0].astype(BF16)
    row = lambda v: v.reshape(1, -1).astype(F32)
    tables = _rope_tables(positions)
    x2d = x.reshape(t, d)

    tm = TM_PROJ
    assert tm == TK, "one projection tile is one attention key block (transposed value tiles)"
    tok = lambda width: pl.BlockSpec((tm, width), lambda i: (i, 0))
    bf = lambda width: jax.ShapeDtypeStruct((t, width), BF16)
    tok_t = lambda width: pl.BlockSpec((None, width, tm), lambda i: (i, 0, 0))
    bf_t = lambda width: jax.ShapeDtypeStruct((t // tm, width, tm), BF16)
    qa, ka, vat, qb, kvb, kvbt, kpe, sg = pl.pallas_call(
        _inproj_kernel,
        grid=(t // tm,),
        in_specs=[tok(d), _full((1, d)), _full(wqkv.shape), _full(wc.shape), _full(wg.shape),
                  _full(wqup.shape), _full(wkvup.shape), _full((1, Q_LORA)), _full((1, KV_LORA))]
                 + [tok(LANES)] * 6,
        out_specs=[tok(A_WIDTH), tok(A_WIDTH), tok_t(A_WIDTH), tok(B_HEADS * LANES),
                   tok(B_HEADS * LANES), tok_t(B_HEADS * LANES), tok(LANES), tok(2 * d)],
        out_shape=[bf(A_WIDTH), bf(A_WIDTH), bf_t(A_WIDTH), bf(B_HEADS * LANES),
                   bf(B_HEADS * LANES), bf_t(B_HEADS * LANES), bf(LANES), bf(2 * d)],
        compiler_params=_params("parallel"),
        name="inproj",
    )(x2d, row(mix_norm_g[0]), wqkv, wc, wg, wqup, wkvup, row(q_norm_g[0]), row(kv_norm_g[0]),
      *tables)

    r3 = lambda a: a.reshape(bsz, seq, a.shape[-1])
    r4t = lambda a: a.reshape(bsz, seq // tm, a.shape[1], tm)
    nblk = seq // TK
    assert nblk <= LANES
    pairs = A_WIDTH // LANES
    ya = pl.pallas_call(
        _moba_kernel,
        grid=(bsz, pairs, seq // TQA),
        in_specs=[pl.BlockSpec((None, TQA, LANES), lambda b, g, i: (b, i, g)),
                  pl.BlockSpec((None, seq, LANES), lambda b, g, i: (b, 0, g)),
                  pl.BlockSpec((None, nblk, LANES, TK), lambda b, g, i: (b, 0, g, 0))],
        out_specs=pl.BlockSpec((None, TQA, LANES), lambda b, g, i: (b, i, g)),
        out_shape=jax.ShapeDtypeStruct((bsz, seq, A_WIDTH), BF16),
        scratch_shapes=[pltpu.VMEM((LANES, LANES), BF16), pltpu.VMEM((LANES, LANES), BF16),
                        pltpu.VMEM((2, LANES, TQA), F32)],
        compiler_params=_params("parallel", "parallel", "arbitrary"),
        name="moba",
    )(r3(qa), r3(ka), r4t(vat))

    yb = pl.pallas_call(
        _mla_kernel,
        grid=(bsz, B_HEADS // 2, seq // TQA),
        in_specs=[pl.BlockSpec((None, TQA, 2 * LANES), lambda b, g, i: (b, i, g)),
                  pl.BlockSpec((None, seq, 2 * LANES), lambda b, g, i: (b, 0, g)),
                  pl.BlockSpec((None, seq, LANES), lambda b, g, i: (b, 0, 0)),
                  pl.BlockSpec((None, nblk, 2 * LANES, TK), lambda b, g, i: (b, 0, g, 0))],
        out_specs=pl.BlockSpec((None, TQA, LANES), lambda b, g, i: (b, i, g)),
        out_shape=jax.ShapeDtypeStruct((bsz, seq, B_HEADS * B_V), BF16),
        scratch_shapes=[pltpu.VMEM((2, LANES, TQA), F32)],
        compiler_params=_params("parallel", "parallel", "arbitrary"),
        name="mla",
    )(r3(qb), r3(kvb), r3(kpe), r4t(kvbt))

    wba = w_branch_a[0].astype(BF16)
    wbb = w_branch_b[0].astype(BF16)
    wout = w_out[0].astype(BF16)
    wpq = w_peer_query[0].astype(BF16)
    sk1 = peer_sub_keys_1[0].astype(BF16)
    sk2 = peer_sub_keys_2[0].astype(BF16)
    score_spec = pl.BlockSpec((PEER_HEADS, PEER_NKEYS, tm), lambda i: (0, 0, i))
    score_shape = jax.ShapeDtypeStruct((PEER_HEADS, PEER_NKEYS, t), F32)
    x1, h2, s1t, s2t = pl.pallas_call(
        _merge_kernel,
        grid=(t // tm,),
        in_specs=[tok(d), tok(A_WIDTH), tok(B_HEADS * B_V), tok(2 * d), _full(wba.shape),
                  _full(wbb.shape), _full(wout.shape), _full((1, d)), _full(wpq.shape),
                  _full(sk1.shape), _full(sk2.shape)],
        out_specs=[tok(d), tok(d), score_spec, score_spec],
        out_shape=[jax.ShapeDtypeStruct((t, d), F32), bf(d), score_shape, score_shape],
        compiler_params=_params("parallel"),
        name="merge",
    )(x2d, ya.reshape(t, A_WIDTH), yb.reshape(t, B_HEADS * B_V), sg, wba, wbb, wout,
      row(ffn_norm_g[0]), wpq, sk1, sk2)

    tl = TL_ROUTE
    hs = pl.BlockSpec((PEER_HEADS, PEER_NKEYS, tl), lambda i: (0, 0, i))
    a_co, e2, tau = pl.pallas_call(
        _route_kernel,
        grid=(t // tl,),
        in_specs=[hs, hs],
        out_specs=[hs, hs, pl.BlockSpec((PEER_HEADS, 1, tl), lambda i: (0, 0, i))],
        out_shape=[score_shape, score_shape, jax.ShapeDtypeStruct((PEER_HEADS, 1, t), F32)],
        scratch_shapes=[pltpu.VMEM((PEER_TOPK, tl), F32)] * 3,
        compiler_params=_params("parallel"),
        name="route",
    )(s1t, s2t)

    u_bf = peer_expert_u[0].astype(BF16)
    vt_bf = peer_expert_v[0].T.astype(BF16)
    tt, te = TT_EXP, TE_EXP
    rt = pl.BlockSpec((PEER_HEADS, PEER_NKEYS, tt), lambda i, e: (0, 0, i))
    n_tiles = n_exp // te
    out = pl.pallas_call(
        functools.partial(_expert_kernel, n_tiles=n_tiles),
        grid=(t // tt, n_tiles + 1),
        in_specs=[pl.BlockSpec((tt, d), lambda i, e: (i, 0)),
                  pl.BlockSpec((te, d), lambda i, e: (jnp.minimum(e, n_tiles - 1), 0)),
                  pl.BlockSpec((d, te), lambda i, e: (0, jnp.maximum(e - 1, 0))),
                  rt, rt, rt, rt,
                  pl.BlockSpec((PEER_HEADS, 1, tt), lambda i, e: (0, 0, i)),
                  pl.BlockSpec((tt, d), lambda i, e: (i, 0)),
                  pl.BlockSpec((1, d), lambda i, e: (0, 0))],
        out_specs=pl.BlockSpec((tt, d), lambda i, e: (i, 0)),
        out_shape=jax.ShapeDtypeStruct((t, d), F32),
        scratch_shapes=[pltpu.VMEM((d, tt), F32), pltpu.VMEM((te, tt), F32),
                        pltpu.VMEM((te, tt), F32), pltpu.VMEM((te, tt), BF16)],
        compiler_params=_params("parallel", "arbitrary"),
        name="experts",
    )(h2, u_bf, vt_bf, s1t, s2t, a_co, e2, tau, x1, row(final_norm_g))
    return out.reshape(bsz, seq, d)
```

```python
import functools
import math

import jax
import jax.numpy as jnp
from jax import lax
from jax.experimental import pallas as pl
from jax.experimental.pallas import tpu as pltpu

F32 = jnp.float32
BF16 = jnp.bfloat16

D_MODEL = 1024
A_HEADS = 8
A_HEAD_DIM = 64
A_WIDTH = A_HEADS * A_HEAD_DIM
MOBA_BLOCK = 256
MOBA_TOPK = 3
B_HEADS = 8
B_NOPE = 64
B_ROPE = 32
B_V = 64
Q_LORA = 256
KV_LORA = 128
PEER_HEADS = 8
PEER_NKEYS = 128
PEER_HALF = 128
PEER_TOPK = 16
ROPE_THETA = 10000.0
RMS_EPS = 1e-6

LANES = 128
NEG_BIG = -1e30
LOG2E = math.log2(math.e)
VMEM_LIMIT = 56 * 1024 * 1024

TM_PROJ = 256
TK = MOBA_BLOCK
TQA = 16 * TK
TL_ROUTE = 256
TT_EXP = 512
TE_EXP = 512
TB_EXP = 256
TJ_EXP = 32


def _dot(a, b):
    return jnp.dot(a, b, preferred_element_type=F32)


def _dot_nt(a, b):
    return lax.dot_general(a, b, (((1,), (1,)), ((), ())), preferred_element_type=F32)


def _rms(x, g):
    return x * lax.rsqrt(jnp.mean(x * x, axis=-1, keepdims=True) + RMS_EPS) * g


def _inproj_kernel(x_ref, g_ref, wqkv_ref, wc_ref, wg_ref, wqup_ref, wkvup_ref, qng_ref,
                   kvng_ref, ca_ref, sa1_ref, sa2_ref, cb_ref, sb1_ref, sb2_ref,
                   qa_ref, ka_ref, vat_ref, qb_ref, kvb_ref, kvbt_ref, kpe_ref, sg_ref):
    h = _rms(x_ref[...], g_ref[...]).astype(BF16)

    qkv = _dot(h, wqkv_ref[...])
    ca, sa1, sa2 = ca_ref[...], sa1_ref[...], sa2_ref[...]

    def rope_a(xg):
        return xg * ca + pltpu.roll(xg, LANES - 32, 1) * sa1 + pltpu.roll(xg, 32, 1) * sa2

    for grp in range(A_WIDTH // LANES):
        lo = grp * LANES
        qa_ref[:, lo:lo + LANES] = (rope_a(qkv[:, lo:lo + LANES]) * (0.125 * LOG2E)).astype(BF16)
        ka_ref[:, lo:lo + LANES] = rope_a(qkv[:, A_WIDTH + lo:A_WIDTH + lo + LANES]).astype(BF16)
    vat_ref[...] = qkv[:, 2 * A_WIDTH:].T.astype(BF16)

    c = _dot(h, wc_ref[...])
    cb, sb1, sb2 = cb_ref[...], sb1_ref[...], sb2_ref[...]

    def rope_b(xg):
        return xg * cb + pltpu.roll(xg, LANES - 16, 1) * sb1 + pltpu.roll(xg, 16, 1) * sb2

    cq = _rms(c[:, :Q_LORA], qng_ref[...]).astype(BF16)
    qb = _dot(cq, wqup_ref[...]) * (LOG2E / math.sqrt(B_NOPE + B_ROPE))
    for hd in range(B_HEADS):
        lo = hd * LANES
        qb_ref[:, lo:lo + LANES] = rope_b(qb[:, lo:lo + LANES]).astype(BF16)
    ckv = _rms(c[:, Q_LORA:Q_LORA + KV_LORA], kvng_ref[...]).astype(BF16)
    kv = _dot(ckv, wkvup_ref[...])
    kvb_ref[...] = kv.astype(BF16)
    kvbt_ref[...] = kv.T.astype(BF16)
    kpe_ref[...] = rope_b(c[:, Q_LORA + KV_LORA:]).astype(BF16)

    sg_ref[...] = jax.nn.sigmoid(_dot(h, wg_ref[...])).astype(BF16)


def _attn_step(scores, carry, acc_ref, vts, c0=0):
    new, ps, alphas = [], [], []
    for s, (m, l) in zip(scores, carry):
        m_new = jnp.maximum(m[:, c0:], jnp.max(s, axis=0, keepdims=True))
        alpha = jnp.exp2(m[:, c0:] - m_new)
        p = jnp.exp2(s - m_new)
        l_new = alpha * l[:, c0:] + jnp.sum(p, axis=0, keepdims=True)
        if c0:
            m_new = jnp.concatenate([m[:, :c0], m_new], axis=1)
            l_new = jnp.concatenate([l[:, :c0], l_new], axis=1)
        new.append((m_new, l_new))
        ps.append(p.astype(BF16))
        alphas.append(alpha)
    for hh in range(2):
        acc_ref[hh, :, c0:] = alphas[hh] * acc_ref[hh, :, c0:] + _dot(vts[hh], ps[hh])
    return tuple(new)


def _attn_loop(i, step):
    per = TQA // TK
    init = tuple((jnp.full((1, TQA), NEG_BIG, F32), jnp.zeros((1, TQA), F32)) for _ in range(2))
    carry = lax.fori_loop(0, per * i, lambda j, c: step(j, c, None, 0), init)
    for a in range(per):
        c0 = a * TK
        krow = lax.broadcasted_iota(jnp.int32, (TK, TQA - c0), 0)
        qcol = lax.broadcasted_iota(jnp.int32, (TK, TQA - c0), 1)
        carry = step(per * i + a, carry, krow <= qcol, c0)
    return carry


def _transpose_bf16(x):
    return x.astype(F32).T.astype(BF16)


def _moba_kernel(q_ref, k_ref, vt_ref, o_ref, kmh_ref, kml_ref, acc_ref):
    i = pl.program_id(2)
    nblk = k_ref.shape[0] // TK

    @pl.when(i == 0)
    def _():
        rows = lax.broadcasted_iota(jnp.int32, (LANES, LANES), 0)
        km = jnp.zeros((LANES, LANES), F32)
        for j in range(nblk):
            blk = k_ref[j * TK:(j + 1) * TK, :].astype(F32)
            km = jnp.where(rows == j, jnp.mean(blk, axis=0, keepdims=True), km)
        hi = km.astype(BF16)
        kmh_ref[...] = hi
        kml_ref[...] = (km - hi.astype(F32)).astype(BF16)

    qt = _transpose_bf16(q_ref[...])
    sub = lax.broadcasted_iota(jnp.int32, (LANES, TQA), 0)
    col = lax.broadcasted_iota(jnp.int32, (LANES, TQA), 1)
    own = (TQA // TK) * i + col // TK
    zero = jnp.zeros_like(qt)
    qt_heads = (jnp.where(sub < A_HEAD_DIM, qt, zero), jnp.where(sub >= A_HEAD_DIM, qt, zero))

    q_aug = []
    for qh in qt_heads:
        gate = _dot(kmh_ref[...], qh) + _dot(kml_ref[...], qh)
        valid = sub < own
        g = jnp.where(valid, gate, -jnp.inf)[:nblk]
        thr = jnp.max(g, axis=0, keepdims=True)
        for _ in range(MOBA_TOPK - 1):
            g = jnp.where(g >= thr, -jnp.inf, g)
            thr = jnp.max(g, axis=0, keepdims=True)
        keep = (valid & (gate >= thr)) | (sub == own)
        bias = jnp.where(keep, 0.0, NEG_BIG).astype(BF16)
        q_aug.append(jnp.concatenate([qh, bias], axis=0))

    lane = lax.broadcasted_iota(jnp.int32, (TK, LANES), 1)
    acc_ref[...] = jnp.zeros_like(acc_ref)

    def step(j, carry, causal, c0):
        start = pl.multiple_of(j * TK, TK)
        onehot = jnp.where(lane == j, 1.0, 0.0).astype(BF16)
        ka = jnp.concatenate([k_ref[pl.ds(start, TK), :], onehot], axis=1)
        scores = [_dot(ka, qa[:, c0:]) for qa in q_aug]
        if causal is not None:
            scores = [jnp.where(causal, s, NEG_BIG) for s in scores]
        vt = vt_ref[j]
        return _attn_step(scores, carry, acc_ref, (vt, vt), c0)

    carry = _attn_loop(i, step)
    outs = [acc_ref[hh] / carry[hh][1] for hh in range(2)]
    o_ref[...] = jnp.where(sub < A_HEAD_DIM, outs[0], outs[1]).T.astype(o_ref.dtype)


def _mla_kernel(q_ref, kv_ref, kpe_ref, kvt_ref, o_ref, acc_ref):
    i = pl.program_id(2)
    lane = lax.broadcasted_iota(jnp.int32, (TK, LANES), 1)
    qts = (_transpose_bf16(q_ref[:, :LANES]), _transpose_bf16(q_ref[:, LANES:]))
    acc_ref[...] = jnp.zeros_like(acc_ref)

    def step(j, carry, causal, c0):
        start = pl.multiple_of(j * TK, TK)
        kpe = kpe_ref[pl.ds(start, TK), :]
        scores = []
        for hh in range(2):
            kv = kv_ref[pl.ds(start, TK), hh * LANES:(hh + 1) * LANES]
            key = jnp.where(lane < B_NOPE, kv, kpe)
            scores.append(_dot(key, qts[hh][:, c0:]))
        if causal is not None:
            scores = [jnp.where(causal, s, NEG_BIG) for s in scores]
        return _attn_step(scores, carry, acc_ref,
                          (kvt_ref[j, :LANES, :], kvt_ref[j, LANES:, :]), c0)

    carry = _attn_loop(i, step)
    outs = [(acc_ref[hh] / carry[hh][1])[B_NOPE:, :] for hh in range(2)]
    o_ref[...] = jnp.concatenate(outs, axis=0).T.astype(o_ref.dtype)


def _merge_kernel(x_ref, ya_ref, yb_ref, sg_ref, wba_ref, wbb_ref, wout_ref, fg_ref, wpq_ref,
                  sk1_ref, sk2_ref, x1_ref, h2_ref, s1_ref, s2_ref):
    merged = (sg_ref[:, :D_MODEL].astype(F32) * _dot(ya_ref[...], wba_ref[...])
              + sg_ref[:, D_MODEL:].astype(F32) * _dot(yb_ref[...], wbb_ref[...]))
    x1 = x_ref[...] + _dot(merged.astype(BF16), wout_ref[...])
    x1_ref[...] = x1
    h2 = _rms(x1, fg_ref[...]).astype(BF16)
    h2_ref[...] = h2
    qp = _dot(h2, wpq_ref[...]).astype(BF16)
    for hd in range(PEER_HEADS):
        lo = 2 * hd * PEER_HALF
        s1_ref[hd] = _dot_nt(sk1_ref[hd], qp[:, lo:lo + PEER_HALF])
        s2_ref[hd] = _dot_nt(sk2_ref[hd], qp[:, lo + PEER_HALF:lo + 2 * PEER_HALF])


def _route_kernel(s1_ref, s2_ref, a_ref, e2_ref, tau_ref, v1_ref, v2_ref, best_ref):
    def top16(src, dst_ref):
        cur = src
        for r in range(PEER_TOPK):
            mx = jnp.max(cur, axis=0, keepdims=True)
            dst_ref[r:r + 1, :] = mx
            cur = jnp.where(cur == mx, -jnp.inf, cur)

    def head(hd, carry):
        s1 = s1_ref[hd]
        s2 = s2_ref[hd]
        top16(s1, v1_ref)
        top16(s2, v2_ref)
        v1 = v1_ref[...]
        v2 = v2_ref[...]
        half = PEER_TOPK // 2
        cand = jnp.concatenate(
            [v1 + v2[0:1]] + [v1[:half] + v2[r:r + 1] for r in range(1, half)]
            + [v1[0:1] + v2[half:]], axis=0)
        top16(cand, best_ref)
        best = best_ref[...]
        z = jnp.sum(jnp.exp(best - best[0:1, :]), axis=0, keepdims=True)
        tau_ref[hd] = best[PEER_TOPK - 1:PEER_TOPK, :]
        a_ref[hd] = jnp.exp(s1 - v1_ref[0:1, :]) * (0.5 / z)
        e2_ref[hd] = jnp.exp(s2 - v2_ref[0:1, :])
        return carry

    lax.fori_loop(0, PEER_HEADS, head, 0)


def _expert_kernel(h2_ref, u_ref, vt_ref, s1_ref, s2_ref, a_ref, e2_ref, tau_ref, x1_ref, fg_ref,
                   o_ref, acc_ref, act0_ref, act1_ref, wa_ref, *, n_tiles):
    e = pl.program_id(1)
    slabs = TE_EXP // PEER_NKEYS
    prev = jnp.maximum(e - 1, 0)

    def body(act_w_ref, act_r_ref):
        for tb in range(TT_EXP // TB_EXP):
            ts = slice(tb * TB_EXP, (tb + 1) * TB_EXP)
            if act_w_ref is not None:
                pre = _dot_nt(u_ref[...], h2_ref[ts, :])
                act_w_ref[:, ts] = pre * (1.0 + lax.erf(pre * math.sqrt(0.5)))
            if act_r_ref is None:
                continue
            tau = [tau_ref[hd, :, ts] for hd in range(PEER_HEADS)]
            for sl in range(slabs):
                i1 = prev * slabs + sl
                s1r = [s1_ref[hd, pl.ds(i1, 1), ts] for hd in range(PEER_HEADS)]
                ar = [a_ref[hd, pl.ds(i1, 1), ts] for hd in range(PEER_HEADS)]
                for jt in range(PEER_NKEYS // TJ_EXP):
                    rj = slice(jt * TJ_EXP, (jt + 1) * TJ_EXP)
                    w = None
                    for hd in range(PEER_HEADS):
                        cand = s2_ref[hd, rj, ts] + s1r[hd]
                        term = jnp.where(cand >= tau[hd], e2_ref[hd, rj, ts], 0.0) * ar[hd]
                        w = term if w is None else w + term
                    ro = slice(sl * PEER_NKEYS + jt * TJ_EXP, sl * PEER_NKEYS + (jt + 1) * TJ_EXP)
                    wa_ref[ro, ts] = (w * act_r_ref[ro, ts]).astype(BF16)
            acc_ref[:, ts] += _dot(vt_ref[...], wa_ref[:, ts])

    acts = (act0_ref, act1_ref)
    inner = jnp.logical_and(e > 0, e < n_tiles)

    @pl.when(e == 0)
    def _():
        acc_ref[...] = jnp.zeros_like(acc_ref)
        body(acts[0], None)

    @pl.when(jnp.logical_and(inner, e % 2 == 0))
    def _():
        body(acts[0], acts[1])

    @pl.when(jnp.logical_and(inner, e % 2 == 1))
    def _():
        body(acts[1], acts[0])

    @pl.when(e == n_tiles)
    def _():
        body(None, acts[(n_tiles - 1) % 2])
        o_ref[...] = _rms(x1_ref[...] + acc_ref[...].T, fg_ref[...])


def _params(*sem):
    return pltpu.CompilerParams(dimension_semantics=sem, vmem_limit_bytes=VMEM_LIMIT)


def _full(shape):
    nd = len(shape)
    return pl.BlockSpec(shape, lambda *_: (0,) * nd)


def _rope_tables(positions):
    pos = positions.reshape(-1).astype(F32)[:, None]

    def cs(dim):
        inv_freq = ROPE_THETA ** (-jnp.arange(0, dim, 2, dtype=F32) / dim)
        ang = pos * inv_freq
        return jnp.cos(ang), jnp.sin(ang)

    t = pos.shape[0]

    def tiled(a):
        n = a.shape[1]
        return jnp.broadcast_to(a[:, None, :], (t, LANES // n, n)).reshape(t, LANES)

    def lanes(*runs):
        return jnp.asarray([[v for v, n in runs for _ in range(n)]], F32)

    ca, sa = cs(A_HEAD_DIM)
    ha = A_HEAD_DIM // 2
    cb, sb = cs(B_ROPE)
    hb = B_ROPE // 2
    rest = LANES - B_NOPE - B_ROPE
    rot = lanes((0.0, B_NOPE), (1.0, B_ROPE), (0.0, rest))
    return (tiled(ca),
            tiled(sa) * lanes((-1.0, ha), (0.0, ha), (-1.0, ha), (0.0, ha)),
            tiled(sa) * lanes((0.0, ha), (1.0, ha), (0.0, ha), (1.0, ha)),
            tiled(cb) * rot + (1.0 - rot),
            tiled(sb) * lanes((0.0, B_NOPE), (-1.0, hb), (0.0, hb), (0.0, rest)),
            tiled(sb) * lanes((0.0, B_NOPE), (0.0, hb), (1.0, hb), (0.0, rest)))


def kernel(x, positions, mix_norm_g, w_in, q_norm_g, w_q_up, kv_norm_g, w_kv_up, w_branch_a,
           w_branch_b, w_out, ffn_norm_g, w_peer_query, peer_sub_keys_1, peer_sub_keys_2,
           peer_expert_u, peer_expert_v, final_norm_g):
    bsz, seq, d = x.shape
    t = bsz * seq
    n_exp = peer_expert_u.shape[1]
    assert d == D_MODEL and seq % TQA == 0 and t % TT_EXP == 0 and n_exp % TE_EXP == 0
    assert mix_norm_g.shape[0] == 1, "single-layer block"

    w = w_in[0]
    o_cq = 3 * A_WIDTH
    o_ckv = o_cq + Q_LORA
    o_kpe = o_ckv + KV_LORA
    o_g = o_kpe + B_ROPE
    wqkv = w[:, :o_cq].astype(BF16)
    wc = jnp.concatenate([w[:, o_cq:o_kpe], jnp.zeros((d, B_NOPE), F32), w[:, o_kpe:o_g],
                          jnp.zeros((d, LANES - B_NOPE - B_ROPE), F32)], axis=1).astype(BF16)
    wg = w[:, o_g:].astype(BF16)
    wqup = jnp.pad(w_q_up[0].reshape(Q_LORA, B_HEADS, B_NOPE + B_ROPE),
                   ((0, 0), (0, 0), (0, LANES - B_NOPE - B_ROPE))).reshape(Q_LORA, B_HEADS * LANES)
    wqup = wqup.astype(BF16)
    wkvup = w_kv_up[0].astype(BF16)
    row = lambda v: v.reshape(1, -1).astype(F32)
    tables = _rope_tables(positions)
    x2d = x.reshape(t, d)

    tm = TM_PROJ
    assert tm == TK, "one projection tile is one attention key block (transposed value tiles)"
    tok = lambda width: pl.BlockSpec((tm, width), lambda i: (i, 0))
    bf = lambda width: jax.ShapeDtypeStruct((t, width), BF16)
    tok_t = lambda width: pl.BlockSpec((None, width, tm), lambda i: (i, 0, 0))
    bf_t = lambda width: jax.ShapeDtypeStruct((t // tm, width, tm), BF16)
    qa, ka, vat, qb, kvb, kvbt, kpe, sg = pl.pallas_call(
        _inproj_kernel,
        grid=(t // tm,),
        in_specs=[tok(d), _full((1, d)), _full(wqkv.shape), _full(wc.shape), _full(wg.shape),
                  _full(wqup.shape), _full(wkvup.shape), _full((1, Q_LORA)), _full((1, KV_LORA))]
                 + [tok(LANES)] * 6,
        out_specs=[tok(A_WIDTH), tok(A_WIDTH), tok_t(A_WIDTH), tok(B_HEADS * LANES),
                   tok(B_HEADS * LANES), tok_t(B_HEADS * LANES), tok(LANES), tok(2 * d)],
        out_shape=[bf(A_WIDTH), bf(A_WIDTH), bf_t(A_WIDTH), bf(B_HEADS * LANES),
                   bf(B_HEADS * LANES), bf_t(B_HEADS * LANES), bf(LANES), bf(2 * d)],
        compiler_params=_params("parallel"),
        name="inproj",
    )(x2d, row(mix_norm_g[0]), wqkv, wc, wg, wqup, wkvup, row(q_norm_g[0]), row(kv_norm_g[0]),
      *tables)

    r3 = lambda a: a.reshape(bsz, seq, a.shape[-1])
    r4t = lambda a: a.reshape(bsz, seq // tm, a.shape[1], tm)
    nblk = seq // TK
    assert nblk <= LANES
    pairs = A_WIDTH // LANES
    ya = pl.pallas_call(
        _moba_kernel,
        grid=(bsz, pairs, seq // TQA),
        in_specs=[pl.BlockSpec((None, TQA, LANES), lambda b, g, i: (b, i, g)),
                  pl.BlockSpec((None, seq, LANES), lambda b, g, i: (b, 0, g)),
                  pl.BlockSpec((None, nblk, LANES, TK), lambda b, g, i: (b, 0, g, 0))],
        out_specs=pl.BlockSpec((None, TQA, LANES), lambda b, g, i: (b, i, g)),
        out_shape=jax.ShapeDtypeStruct((bsz, seq, A_WIDTH), BF16),
        scratch_shapes=[pltpu.VMEM((LANES, LANES), BF16), pltpu.VMEM((LANES, LANES), BF16),
                        pltpu.VMEM((2, LANES, TQA), F32)],
        compiler_params=_params("parallel", "parallel", "arbitrary"),
        name="moba",
    )(r3(qa), r3(ka), r4t(vat))

    yb = pl.pallas_call(
        _mla_kernel,
        grid=(bsz, B_HEADS // 2, seq // TQA),
        in_specs=[pl.BlockSpec((None, TQA, 2 * LANES), lambda b, g, i: (b, i, g)),
                  pl.BlockSpec((None, seq, 2 * LANES), lambda b, g, i: (b, 0, g)),
                  pl.BlockSpec((None, seq, LANES), lambda b, g, i: (b, 0, 0)),
                  pl.BlockSpec((None, nblk, 2 * LANES, TK), lambda b, g, i: (b, 0, g, 0))],
        out_specs=pl.BlockSpec((None, TQA, LANES), lambda b, g, i: (b, i, g)),
        out_shape=jax.ShapeDtypeStruct((bsz, seq, B_HEADS * B_V), BF16),
        scratch_shapes=[pltpu.VMEM((2, LANES, TQA), F32)],
        compiler_params=_params("parallel", "parallel", "arbitrary"),
        name="mla",
    )(r3(qb), r3(kvb), r3(kpe), r4t(kvbt))

    wba = w_branch_a[0].astype(BF16)
    wbb = w_branch_b[0].astype(BF16)
    wout = w_out[0].astype(BF16)
    wpq = w_peer_query[0].astype(BF16)
    sk1 = peer_sub_keys_1[0].astype(BF16)
    sk2 = peer_sub_keys_2[0].astype(BF16)
    score_spec = pl.BlockSpec((PEER_HEADS, PEER_NKEYS, tm), lambda i: (0, 0, i))
    score_shape = jax.ShapeDtypeStruct((PEER_HEADS, PEER_NKEYS, t), F32)
    x1, h2, s1t, s2t = pl.pallas_call(
        _merge_kernel,
        grid=(t // tm,),
        in_specs=[tok(d), tok(A_WIDTH), tok(B_HEADS * B_V), tok(2 * d), _full(wba.shape),
                  _full(wbb.shape), _full(wout.shape), _full((1, d)), _full(wpq.shape),
                  _full(sk1.shape), _full(sk2.shape)],
        out_specs=[tok(d), tok(d), score_spec, score_spec],
        out_shape=[jax.ShapeDtypeStruct((t, d), F32), bf(d), score_shape, score_shape],
        compiler_params=_params("parallel"),
        name="merge",
    )(x2d, ya.reshape(t, A_WIDTH), yb.reshape(t, B_HEADS * B_V), sg, wba, wbb, wout,
      row(ffn_norm_g[0]), wpq, sk1, sk2)

    tl = TL_ROUTE
    hs = pl.BlockSpec((PEER_HEADS, PEER_NKEYS, tl), lambda i: (0, 0, i))
    a_co, e2, tau = pl.pallas_call(
        _route_kernel,
        grid=(t // tl,),
        in_specs=[hs, hs],
        out_specs=[hs, hs, pl.BlockSpec((PEER_HEADS, 1, tl), lambda i: (0, 0, i))],
        out_shape=[score_shape, score_shape, jax.ShapeDtypeStruct((PEER_HEADS, 1, t), F32)],
        scratch_shapes=[pltpu.VMEM((PEER_TOPK, tl), F32)] * 3,
        compiler_params=_params("parallel"),
        name="route",
    )(s1t, s2t)

    u_bf = peer_expert_u[0].astype(BF16)
    vt_bf = peer_expert_v[0].T.astype(BF16)
    tt, te = TT_EXP, TE_EXP
    rt = pl.BlockSpec((PEER_HEADS, PEER_NKEYS, tt), lambda i, e: (0, 0, i))
    n_tiles = n_exp // te
    out = pl.pallas_call(
        functools.partial(_expert_kernel, n_tiles=n_tiles),
        grid=(t // tt, n_tiles + 1),
        in_specs=[pl.BlockSpec((tt, d), lambda i, e: (i, 0)),
                  pl.BlockSpec((te, d), lambda i, e: (jnp.minimum(e, n_tiles - 1), 0)),
                  pl.BlockSpec((d, te), lambda i, e: (0, jnp.maximum(e - 1, 0))),
                  rt, rt, rt, rt,
                  pl.BlockSpec((PEER_HEADS, 1, tt), lambda i, e: (0, 0, i)),
                  pl.BlockSpec((tt, d), lambda i, e: (i, 0)),
                  pl.BlockSpec((1, d), lambda i, e: (0, 0))],
        out_specs=pl.BlockSpec((tt, d), lambda i, e: (i, 0)),
        out_shape=jax.ShapeDtypeStruct((t, d), F32),
        scratch_shapes=[pltpu.VMEM((d, tt), F32), pltpu.VMEM((te, tt), F32),
                        pltpu.VMEM((te, tt), F32), pltpu.VMEM((te, tt), BF16)],
        compiler_params=_params("parallel", "arbitrary"),
        name="experts",
    )(h2, u_bf, vt_bf, s1t, s2t, a_co, e2, tau, x1, row(final_norm_g))
    return out.reshape(bsz, seq, d)
```
